```python
import math
import jax, jax.numpy as jnp
from jax import lax
import numpy as np

D_MODEL = 4096
BATCH = 4
SEQ = 2048
DEPTH = 4
DEC_BATCH = 128
DEC_SEQ = 4
PAST_LEN = 16384
PAGE_SIZE = 128

N_MIXERS = 3
N_GLA_LAYERS = (DEPTH + 2) // 3
N_RET_LAYERS = (DEPTH + 1) // 3
N_MLSTM_LAYERS = DEPTH // 3

GLA_HEADS = 4
GLA_DK = D_MODEL // 2 // GLA_HEADS
GLA_DV = D_MODEL // GLA_HEADS
GLA_RANK = 16
GLA_TAU = 16.0
RET_HEADS = D_MODEL // 256
RET_DK = D_MODEL // RET_HEADS
RET_DV = 2 * D_MODEL // RET_HEADS
ROPE_BASE = 10000.0
ML_HEADS = 8
ML_DK = D_MODEL // 2 // ML_HEADS
ML_DV = D_MODEL // ML_HEADS
ML_GATE_CAP = 15.0
D_FF = 4 * D_MODEL
CHUNK = 64
EPS = 1e-6

kernel_name = "hybrid_gla_retnet_mlstm_adaln_step"


def _rmsnorm(x, g):
    xf = x.astype(jnp.float32)
    y = xf * lax.rsqrt(jnp.mean(xf * xf, axis=-1, keepdims=True) + EPS)
    return (y * g.astype(jnp.float32)).astype(x.dtype)


def _head_rmsnorm(o, g):
    y = o * lax.rsqrt(jnp.mean(o * o, axis=-1, keepdims=True) + EPS)
    return y if g is None else y * g.astype(jnp.float32)


def _heads(t, n_heads):
    b, l, _ = t.shape
    return t.reshape(b, l, n_heads, -1).transpose(0, 2, 1, 3)


def _merge(t):
    b, h, l, d = t.shape
    return t.transpose(0, 2, 1, 3).reshape(b, l, h * d)


def _chunk_len(length):
    return CHUNK if length % CHUNK == 0 else length


def _split_chunks(t, c):
    b, h, l = t.shape[:3]
    t = t.reshape((b, h, l // c, c) + t.shape[3:])
    return jnp.moveaxis(t, 2, 0)


def _join_chunks(t):
    t = jnp.moveaxis(t, 0, 2)
    s = t.shape
    return t.reshape(s[:2] + (s[2] * s[3],) + s[4:])


def _softcap(z):
    return ML_GATE_CAP * jnp.tanh(z / ML_GATE_CAP)


def _rotary(t, pos):
    half = t.shape[-1] // 2
    inv = ROPE_BASE ** (-jnp.arange(half, dtype=jnp.float32) / half)
    ang = pos[:, None] * inv[None, :]
    cos, sin = jnp.cos(ang), jnp.sin(ang)
    t1, t2 = t[..., :half], t[..., half:]
    return jnp.concatenate([t1 * cos - t2 * sin, t1 * sin + t2 * cos], axis=-1)


def _chunked_gated_linear_attn(q, k, v, log_a, s0):
    c = _chunk_len(q.shape[2])
    mask = jnp.tril(jnp.ones((c, c), dtype=bool))

    def step(s, xs):
        qc, kc, vc, gc = xs
        b = jnp.cumsum(gc, axis=-2)
        b_end = b[..., -1:, :]
        q_dec = qc * jnp.exp(b)
        k_inv = kc * jnp.exp(-b)
        att = jnp.where(mask, jnp.einsum("bhid,bhjd->bhij", q_dec, k_inv), 0.0)
        o = jnp.einsum("bhij,bhjv->bhiv", att, vc) + jnp.einsum("bhid,bhdv->bhiv", q_dec, s)
        k_end = kc * jnp.exp(b_end - b)
        s_new = jnp.swapaxes(jnp.exp(b_end), -1, -2) * s + jnp.einsum("bhjd,bhjv->bhdv", k_end, vc)
        return s_new, o

    s, o = lax.scan(step, s0, (_split_chunks(q, c), _split_chunks(k, c),
                               _split_chunks(v, c), _split_chunks(log_a, c)))
    return _join_chunks(o), s


def _chunked_mlstm(q, k, v, i_pre, log_f, c0, n0, m0):
    c = _chunk_len(q.shape[2])
    mask = jnp.tril(jnp.ones((c, c), dtype=bool))

    def step(carry, xs):
        cm, n, m = carry
        qc, kc, vc, ic, fc = xs
        f_cum = jnp.cumsum(fc, axis=-1)
        logw = jnp.where(mask, f_cum[..., :, None] - f_cum[..., None, :] + ic[..., None, :], -jnp.inf)
        log_inter = f_cum + m[..., None]
        m_t = jnp.maximum(log_inter, jnp.max(logw, axis=-1))
        w = jnp.exp(logw - m_t[..., None])
        a = jnp.exp(log_inter - m_t)
        s = jnp.einsum("bhid,bhjd->bhij", qc, kc) * w
        num = jnp.einsum("bhij,bhjv->bhiv", s, vc) + a[..., None] * jnp.einsum("bhid,bhdv->bhiv", qc, cm)
        den = jnp.sum(s, axis=-1) + a * jnp.einsum("bhid,bhd->bhi", qc, n)
        hc = num / jnp.maximum(jnp.abs(den), jnp.exp(-m_t))[..., None]
        m_new = m_t[..., -1]
        a_end = jnp.exp(f_cum[..., -1] + m - m_new)
        w_end = jnp.exp(f_cum[..., -1:] - f_cum + ic - m_new[..., None])
        c_new = a_end[..., None, None] * cm + jnp.einsum("bhj,bhjd,bhjv->bhdv", w_end, kc, vc)
        n_new = a_end[..., None] * n + jnp.einsum("bhj,bhjd->bhd", w_end, kc)
        return (c_new, n_new, m_new), hc

    (cm, n, m), h = lax.scan(step, (c0, n0, m0),
                             (_split_chunks(q, c), _split_chunks(k, c), _split_chunks(v, c),
                              _split_chunks(i_pre, c), _split_chunks(log_f, c)))
    return _join_chunks(h), cm, n, m


def _gla_mixer(h, s0, w_q, w_k, w_v, w_a1, w_a2, b_a, w_r, b_r, g_o, w_o):
    f32 = jnp.float32
    q = _heads(h @ w_q, GLA_HEADS).astype(f32) * GLA_DK ** -0.5
    k = _heads(h @ w_k, GLA_HEADS).astype(f32)
    v = _heads(h @ w_v, GLA_HEADS).astype(f32)
    log_a = jax.nn.log_sigmoid(((h @ w_a1) @ w_a2 + b_a).astype(f32)) / GLA_TAU
    log_a = _heads(log_a, GLA_HEADS)
    o, s = _chunked_gated_linear_attn(q, k, v, log_a, s0.astype(f32))
    o = _merge(_head_rmsnorm(o, g_o)).astype(h.dtype)
    out = (o * jax.nn.silu(h @ w_r + b_r)) @ w_o
    return out, s


def _ret_mixer(h, s0, pos0, w_q, w_k, w_v, w_g, w_o):
    f32 = jnp.float32
    length = h.shape[1]
    pos = pos0 + jnp.arange(length, dtype=f32)
    q = _rotary(_heads(h @ w_q, RET_HEADS).astype(f32), pos)
    k = _rotary(_heads(h @ w_k, RET_HEADS).astype(f32), pos) * RET_DK ** -0.5
    v = _heads(h @ w_v, RET_HEADS).astype(f32)
    log_gamma = jnp.log1p(-jnp.exp2(-5.0 - jnp.arange(RET_HEADS, dtype=f32)))
    log_a = jnp.broadcast_to(log_gamma[None, :, None, None], q.shape)
    o, s = _chunked_gated_linear_attn(q, k, v, log_a, s0.astype(f32))
    o = _merge(_head_rmsnorm(o, None)).astype(h.dtype)
    out = (jax.nn.silu(h @ w_g) * o) @ w_o
    return out, s


def _mlstm_mixer(h, c0, n0, m0, w_q, w_k, w_v, w_i, b_i, w_f, b_f, w_og, g_o, w_o):
    f32 = jnp.float32
    q = _heads(h @ w_q, ML_HEADS).astype(f32)
    k = _heads(h @ w_k, ML_HEADS).astype(f32) * ML_DK ** -0.5
    v = _heads(h @ w_v, ML_HEADS).astype(f32)
    i_pre = _softcap((h @ w_i + b_i).astype(f32)).transpose(0, 2, 1)
    log_f = jax.nn.log_sigmoid(_softcap((h @ w_f + b_f).astype(f32))).transpose(0, 2, 1)
    hh, cm, n, m = _chunked_mlstm(q, k, v, i_pre, log_f, c0.astype(f32), n0.astype(f32), m0.astype(f32))
    hh = _merge(_head_rmsnorm(hh, g_o)).astype(h.dtype)
    out = (jax.nn.sigmoid(h @ w_og) * hh) @ w_o
    return out, cm, n, m


def _trunk(x, c, gla_s, ret_s, ml_c, ml_n, ml_m, pos0, p):
    sc = jax.nn.silu(c)
    new_gla, new_ret, new_c, new_n, new_m = [], [], [], [], []
    for i in range(DEPTH):
        kind, j = i % N_MIXERS, i // N_MIXERS
        mod = (sc @ p["w_ada"][i] + p["b_ada"][i])[:, None, :]
        sh1, sc1, gt1, sh2, sc2, gt2 = jnp.split(mod, 6, axis=-1)
        hm = _rmsnorm(x, p["g_mix"][i]) * (1 + sc1) + sh1
        if kind == 0:
            out, s = _gla_mixer(hm, gla_s[j], p["w_gla_q"][j], p["w_gla_k"][j], p["w_gla_v"][j],
                                p["w_gla_a1"][j], p["w_gla_a2"][j], p["b_gla_a"][j], p["w_gla_r"][j],
                                p["b_gla_r"][j], p["g_gla_o"][j], p["w_gla_o"][j])
            new_gla.append(s.astype(x.dtype))
        elif kind == 1:
            out, s = _ret_mixer(hm, ret_s[j], pos0, p["w_ret_q"][j], p["w_ret_k"][j], p["w_ret_v"][j],
                                p["w_ret_g"][j], p["w_ret_o"][j])
            new_ret.append(s.astype(x.dtype))
        else:
            out, cm, n, m = _mlstm_mixer(hm, ml_c[j], ml_n[j], ml_m[j], p["w_ml_q"][j], p["w_ml_k"][j],
                                         p["w_ml_v"][j], p["w_ml_i"][j], p["b_ml_i"][j], p["w_ml_f"][j],
                                         p["b_ml_f"][j], p["w_ml_og"][j], p["g_ml_o"][j], p["w_ml_o"][j])
            new_c.append(cm.astype(x.dtype))
            new_n.append(n.astype(x.dtype))
            new_m.append(m.astype(x.dtype))
        x = x + gt1 * out
        hf = _rmsnorm(x, p["g_mlp"][i]) * (1 + sc2) + sh2
        x = x + gt2 * (jnp.square(jax.nn.relu(hf @ p["w_up"][i])) @ p["w_down"][i])
    y = _rmsnorm(x, p["g_final"])
    return (y, jnp.stack(new_gla), jnp.stack(new_ret), jnp.stack(new_c), jnp.stack(new_n), jnp.stack(new_m))


def setup_inputs(seed: int = 0) -> dict:
    key = jax.random.key(seed)
    ks = iter(jax.random.split(key, 48))
    f32 = jnp.float32
    D = D_MODEL
    NG, NR, NM = N_GLA_LAYERS, N_RET_LAYERS, N_MLSTM_LAYERS

    def nrm(shape, scale=1.0):
        return jax.random.normal(next(ks), shape, f32) * scale

    def gain(shape):
        return 1.0 + nrm(shape, 0.02)

    gk, gv = GLA_HEADS * GLA_DK, GLA_HEADS * GLA_DV
    rk, rv = RET_HEADS * RET_DK, RET_HEADS * RET_DV
    mk, mv = ML_HEADS * ML_DK, ML_HEADS * ML_DV
    return {
        "x_prompt": nrm((BATCH, SEQ, D)),
        "x_sample": nrm((DEC_BATCH, DEC_SEQ, D)),
        "state_gla": nrm((NG, DEC_BATCH, GLA_HEADS, GLA_DK, GLA_DV), 0.5),
        "state_ret": nrm((NR, DEC_BATCH, RET_HEADS, RET_DK, RET_DV), 0.5),
        "state_mlstm_C": nrm((NM, DEC_BATCH, ML_HEADS, ML_DK, ML_DV), 0.5),
        "state_mlstm_n": nrm((NM, DEC_BATCH, ML_HEADS, ML_DK), 0.5),
        "state_mlstm_m": nrm((NM, DEC_BATCH, ML_HEADS)),
        "c_prompt": nrm((BATCH, D)),
        "c_sample": nrm((DEC_BATCH, D)),
        "w_ada": nrm((DEPTH, D, 6 * D), D ** -0.5),
        "b_ada": nrm((DEPTH, 6 * D), 0.02),
        "g_mix": gain((DEPTH, D)),
        "g_mlp": gain((DEPTH, D)),
        "w_up": nrm((DEPTH, D, D_FF), D ** -0.5),
        "w_down": nrm((DEPTH, D_FF, D), D_FF ** -0.5),
        "g_final": gain((D,)),
        "w_gla_q": nrm((NG, D, gk), D ** -0.5),
        "w_gla_k": nrm((NG, D, gk), D ** -0.5),
        "w_gla_v": nrm((NG, D, gv), D ** -0.5),
        "w_gla_a1": nrm((NG, D, GLA_RANK), D ** -0.5),
        "w_gla_a2": nrm((NG, GLA_RANK, gk), GLA_RANK ** -0.5),
        "b_gla_a": nrm((NG, gk), 0.1),
        "w_gla_r": nrm((NG, D, D), D ** -0.5),
        "b_gla_r": nrm((NG, D), 0.02),
        "g_gla_o": gain((NG, GLA_DV)),
        "w_gla_o": nrm((NG, gv, D), gv ** -0.5),
        "w_ret_q": nrm((NR, D, rk), D ** -0.5),
        "w_ret_k": nrm((NR, D, rk), D ** -0.5),
        "w_ret_v": nrm((NR, D, rv), D ** -0.5),
        "w_ret_g": nrm((NR, D, rv), D ** -0.5),
        "w_ret_o": nrm((NR, rv, D), rv ** -0.5),
        "w_ml_q": nrm((NM, D, mk), D ** -0.5),
        "w_ml_k": nrm((NM, D, mk), D ** -0.5),
        "w_ml_v": nrm((NM, D, mv), D ** -0.5),
        "w_ml_i": nrm((NM, D, ML_HEADS), D ** -0.5),
        "b_ml_i": nrm((NM, ML_HEADS), 0.1),
        "w_ml_f": nrm((NM, D, ML_HEADS), D ** -0.5),
        "b_ml_f": 3.0 + nrm((NM, ML_HEADS), 0.5),
        "w_ml_og": nrm((NM, D, mv), D ** -0.5),
        "g_ml_o": gain((NM, ML_DV)),
        "w_ml_o": nrm((NM, mv, D), mv ** -0.5),
    }


def reference(x_prompt, x_sample, state_gla, state_ret, state_mlstm_C, state_mlstm_n, state_mlstm_m,
              c_prompt, c_sample, w_ada, b_ada, g_mix, g_mlp, w_up, w_down, g_final,
              w_gla_q, w_gla_k, w_gla_v, w_gla_a1, w_gla_a2, b_gla_a, w_gla_r, b_gla_r, g_gla_o, w_gla_o,
              w_ret_q, w_ret_k, w_ret_v, w_ret_g, w_ret_o,
              w_ml_q, w_ml_k, w_ml_v, w_ml_i, b_ml_i, w_ml_f, b_ml_f, w_ml_og, g_ml_o, w_ml_o):
    p = dict(w_ada=w_ada, b_ada=b_ada, g_mix=g_mix, g_mlp=g_mlp, w_up=w_up, w_down=w_down, g_final=g_final,
             w_gla_q=w_gla_q, w_gla_k=w_gla_k, w_gla_v=w_gla_v, w_gla_a1=w_gla_a1, w_gla_a2=w_gla_a2,
             b_gla_a=b_gla_a, w_gla_r=w_gla_r, b_gla_r=b_gla_r, g_gla_o=g_gla_o, w_gla_o=w_gla_o,
             w_ret_q=w_ret_q, w_ret_k=w_ret_k, w_ret_v=w_ret_v, w_ret_g=w_ret_g, w_ret_o=w_ret_o,
             w_ml_q=w_ml_q, w_ml_k=w_ml_k, w_ml_v=w_ml_v, w_ml_i=w_ml_i, b_ml_i=b_ml_i, w_ml_f=w_ml_f,
             b_ml_f=b_ml_f, w_ml_og=w_ml_og, g_ml_o=g_ml_o, w_ml_o=w_ml_o)
    f32 = jnp.float32
    bp = x_prompt.shape[0]
    z_gla = jnp.zeros((N_GLA_LAYERS, bp, GLA_HEADS, GLA_DK, GLA_DV), f32)
    z_ret = jnp.zeros((N_RET_LAYERS, bp, RET_HEADS, RET_DK, RET_DV), f32)
    z_c = jnp.zeros((N_MLSTM_LAYERS, bp, ML_HEADS, ML_DK, ML_DV), f32)
    z_n = jnp.zeros((N_MLSTM_LAYERS, bp, ML_HEADS, ML_DK), f32)
    z_m = jnp.zeros((N_MLSTM_LAYERS, bp, ML_HEADS), f32)
    y_prompt, gla_p, ret_p, c_p, n_p, m_p = _trunk(x_prompt, c_prompt, z_gla, z_ret, z_c, z_n, z_m, 0, p)
    y_sample, gla_s, ret_s, c_s, n_s, m_s = _trunk(x_sample, c_sample, state_gla, state_ret, state_mlstm_C,
                                                   state_mlstm_n, state_mlstm_m, PAST_LEN, p)
    return (y_prompt, y_sample, gla_p, gla_s, ret_p, ret_s, c_p, c_s, n_p, n_s, m_p, m_s)
```

```python
import functools

import jax
import jax.numpy as jnp
from jax import lax
from jax.experimental import pallas as pl
from jax.experimental.pallas import tpu as pltpu

F32 = jnp.float32
BF16 = jnp.bfloat16

CHUNK = 64
EPS = 1e-6
GLA_TAU = 16.0
ML_GATE_CAP = 15.0
ROPE_BASE = 10000.0
PAST_LEN = 16384

_VMEM_LIMIT_V7X = 60 * 2**20
_LANES = 128
_SUBLANES = 8
_SAMPLE_PAD = _SUBLANES


def _params(n_grid, limit=_VMEM_LIMIT_V7X):
    return pltpu.CompilerParams(dimension_semantics=("arbitrary",) * n_grid, vmem_limit_bytes=limit)


def _dot(a, b):
    return jnp.dot(a, b, preferred_element_type=F32)


def _dot_nt(a, b):
    return lax.dot_general(a, b, (((1,), (1,)), ((), ())), preferred_element_type=F32)


def _dot_tn(a, b):
    return lax.dot_general(a, b, (((0,), (0,)), ((), ())), preferred_element_type=F32)


def _sigmoid(x):
    return 1.0 / (1.0 + jnp.exp(-x))


def _silu(x):
    return x * _sigmoid(x)


def _log_sigmoid(x):
    return jnp.minimum(x, 0.0) - jnp.log1p(jnp.exp(-jnp.abs(x)))


def _softcap(z):
    return ML_GATE_CAP * jnp.tanh(z / ML_GATE_CAP)


class _Tok:
    def __init__(self, bp, lp, bs, ls, tile):
        self.bp, self.lp, self.bs, self.ls = bp, lp, bs, ls
        self.mp, self.ms = bp * lp, bs * ls
        self.m = self.mp + self.ms
        assert lp % tile == 0 and self.ms % tile == 0, (lp, self.ms, tile)
        self.tile = tile
        self.npt = self.mp // tile
        self.nst = self.ms // tile
        self.tpb = lp // tile

    def with_tile(self, tile):
        return _Tok(self.bp, self.lp, self.bs, self.ls, tile)

    def prompt_idx(self, i):
        return jnp.minimum(i // self.tpb, self.bp - 1)

    def sample_idx(self, i):
        return jnp.maximum(i - self.npt, 0)


def _ada_body(c_ref, w_ref, b_ref, o_ref):
    sc = _silu(c_ref[...]).astype(BF16)
    o_ref[...] = _dot(sc, w_ref[...].astype(BF16)) + b_ref[...]


def _ada_mod(c_all, w_ada, b_ada, tn):
    depth, d, n = w_ada.shape
    rows = c_all.shape[0]
    return pl.pallas_call(
        _ada_body,
        grid=(depth, n // tn),
        in_specs=[
            pl.BlockSpec((rows, d), lambda l, j: (0, 0)),
            pl.BlockSpec((None, d, tn), lambda l, j: (l, 0, j)),
            pl.BlockSpec((None, 1, tn), lambda l, j: (l, 0, j)),
        ],
        out_specs=pl.BlockSpec((None, rows, tn), lambda l, j: (l, 0, j)),
        out_shape=jax.ShapeDtypeStruct((depth, rows, n), F32),
        compiler_params=_params(2),
        name="ada_mod",
    )(c_all, w_ada, b_ada.reshape(depth, 1, n))


def _rms(x, g):
    return x * lax.rsqrt(jnp.mean(x * x, axis=-1, keepdims=True) + EPS) * g


def _norm_mod_body(x_ref, g_ref, scp_ref, shp_ref, scs_ref, shs_ref, o_ref, *, npt):
    i = pl.program_id(0)
    y = _rms(x_ref[...], g_ref[...])

    @pl.when(i < npt)
    def _():
        o_ref[...] = (y * (1.0 + scp_ref[...]) + shp_ref[...]).astype(o_ref.dtype)

    @pl.when(i >= npt)
    def _():
        o_ref[...] = (y * (1.0 + scs_ref[...]) + shs_ref[...]).astype(o_ref.dtype)


def _norm_mod(x, g, scale, shift, tok):
    m, d = x.shape
    t = tok.tile
    p_spec = pl.BlockSpec((None, 1, d), lambda i: (tok.prompt_idx(i), 0, 0))
    s_spec = pl.BlockSpec((t, d), lambda i: (tok.sample_idx(i), 0))
    return pl.pallas_call(
        functools.partial(_norm_mod_body, npt=tok.npt),
        grid=(m // t,),
        in_specs=[pl.BlockSpec((t, d), lambda i: (i, 0)), pl.BlockSpec((1, d), lambda i: (0, 0)),
                  p_spec, p_spec, s_spec, s_spec],
        out_specs=pl.BlockSpec((t, d), lambda i: (i, 0)),
        out_shape=jax.ShapeDtypeStruct((m, d), BF16),
        compiler_params=_params(1),
        name="norm_mod",
    )(x, g.reshape(1, d), scale[0], shift[0], scale[1], shift[1])


def _final_norm_body(x_ref, g_ref, o_ref):
    o_ref[...] = _rms(x_ref[...], g_ref[...])


def _final_norm(x, g, tile, first_tile, n_tiles):
    d = x.shape[1]
    return pl.pallas_call(
        _final_norm_body,
        grid=(n_tiles,),
        in_specs=[pl.BlockSpec((tile, d), lambda i: (i + first_tile, 0)), pl.BlockSpec((1, d), lambda i: (0, 0))],
        out_specs=pl.BlockSpec((tile, d), lambda i: (i, 0)),
        out_shape=jax.ShapeDtypeStruct((n_tiles * tile, d), F32),
        compiler_params=_params(1),
        name="final_norm",
    )(x, g.reshape(1, d))


def _epi_plain(acc, ex, is_prompt):
    return acc


def _epi_bias_silu(acc, ex, is_prompt):
    return _silu(acc + ex[0][...])


def _epi_silu(acc, ex, is_prompt):
    return _silu(acc)


def _epi_sigmoid(acc, ex, is_prompt):
    return _sigmoid(acc)


def _epi_relu2(acc, ex, is_prompt):
    r = jnp.maximum(acc, 0.0)
    return r * r


def _epi_resid(acc, ex, is_prompt):
    x_ref, gp_ref, gs_ref = ex
    gate = jnp.where(is_prompt, gp_ref[...], gs_ref[...])
    return x_ref[...] + gate * acc


def _mm_body_resident(*refs, epi, n_ex, npt):
    a_ref, w_ref = refs[0], refs[1]
    ex = refs[2:2 + n_ex]
    o_ref, wbf_ref = refs[2 + n_ex], refs[3 + n_ex]
    i = pl.program_id(1)

    @pl.when(i == 0)
    def _():
        wbf_ref[...] = w_ref[...].astype(BF16)

    acc = _dot(a_ref[...], wbf_ref[...])
    o_ref[...] = epi(acc, ex, i < npt).astype(o_ref.dtype)


def _mm_body_ksplit(*refs, epi, n_ex, npt, nk, tm):
    a_ref, w_ref = refs[0], refs[1]
    ex = refs[2:2 + n_ex]
    o_ref, wbf_ref, acc_ref = refs[2 + n_ex], refs[3 + n_ex], refs[4 + n_ex]
    k = pl.program_id(1)
    i = pl.program_id(2)

    @pl.when(i == 0)
    def _():
        wbf_ref[...] = w_ref[...].astype(BF16)

    rows = pl.ds(pl.multiple_of(i * tm, tm), tm)
    p = _dot(a_ref[...], wbf_ref[...])

    @pl.when(k == 0)
    def _():
        acc_ref[rows, :] = p

    @pl.when(jnp.logical_and(k > 0, k < nk - 1))
    def _():
        acc_ref[rows, :] += p

    @pl.when(k == nk - 1)
    def _():
        o_ref[...] = epi(acc_ref[rows, :] + p, ex, i < npt).astype(o_ref.dtype)


def _mm(a, w, layer, tok, *, tn, tk=None, epi=_epi_plain, extras=(), out_dtype=F32, name="mm"):
    m, kdim = a.shape
    n = w.shape[2]
    tm = tok.tile
    tn = min(tn, n)
    tk = kdim if tk is None else tk
    nk = kdim // tk
    assert m % tm == 0 and n % tn == 0 and kdim % tk == 0
    ex_arrays = [e[0] for e in extras]
    if nk == 1:
        grid = (n // tn, m // tm)
        wrap = lambda f: (lambda j, i: f(i, j))
        in_specs = [pl.BlockSpec((tm, kdim), lambda j, i: (i, 0)),
                    pl.BlockSpec((None, kdim, tn), lambda j, i: (layer, 0, j))]
        body = functools.partial(_mm_body_resident, epi=epi, n_ex=len(extras), npt=tok.npt)
        scratch = [pltpu.VMEM((kdim, tn), BF16)]
    else:
        grid = (n // tn, nk, m // tm)
        wrap = lambda f: (lambda j, k, i: f(jnp.where(k == nk - 1, i, 0), j))
        in_specs = [pl.BlockSpec((tm, tk), lambda j, k, i: (i, k)),
                    pl.BlockSpec((None, tk, tn), lambda j, k, i: (layer, k, j))]
        body = functools.partial(_mm_body_ksplit, epi=epi, n_ex=len(extras), npt=tok.npt, nk=nk, tm=tm)
        scratch = [pltpu.VMEM((tk, tn), BF16), pltpu.VMEM((m, tn), F32)]
    in_specs += [pl.BlockSpec(e[1], wrap(e[2])) for e in extras]
    return pl.pallas_call(
        body,
        grid=grid,
        in_specs=in_specs,
        out_specs=pl.BlockSpec((tm, tn), wrap(lambda i, j: (i, j))),
        out_shape=jax.ShapeDtypeStruct((m, n), out_dtype),
        scratch_shapes=scratch,
        compiler_params=_params(len(grid)),
        name=name,
    )(a, w, *ex_arrays)


def _resid_extras(x, gate, tok, tn):
    gp, gs = gate
    return [
        (x, (tok.tile, tn), lambda i, j: (i, j)),
        (gp, (None, 1, tn), lambda i, j: (tok.prompt_idx(i), 0, j)),
        (gs, (tok.tile, tn), lambda i, j: (tok.sample_idx(i), j)),
    ]


_RESID_TN = 512
_KSPLIT_TK = 2048


def _mm_resid(a, w, layer, x, gate, tok, d, name):
    tn = min(_RESID_TN, w.shape[2])
    tk = None if a.shape[1] <= d else min(_KSPLIT_TK, a.shape[1] // 2)
    return _mm(a, w, layer, tok, tn=tn, tk=tk, epi=_epi_resid, extras=_resid_extras(x, gate, tok, tn), name=name)


def _bias_extras(b, tn):
    return [(b.reshape(1, -1), (1, tn), lambda i, j: (0, j))]


def _gla_decay_body(a_ref, w1_ref, w2_ref, b_ref, o_ref):
    t = _dot(a_ref[...], w1_ref[...].astype(BF16)).astype(BF16)
    z = _dot(t, w2_ref[...].astype(BF16)) + b_ref[...]
    o_ref[...] = _log_sigmoid(z) / GLA_TAU


def _gla_decay(h, w1, w2, b, tok):
    m, d = h.shape
    rank, n = w2.shape
    pad = _LANES - rank
    w1p = jnp.pad(w1, ((0, 0), (0, pad)))
    w2p = jnp.pad(w2, ((0, pad), (0, 0)))
    tm = tok.tile
    return pl.pallas_call(
        _gla_decay_body,
        grid=(m // tm,),
        in_specs=[pl.BlockSpec((tm, d), lambda i: (i, 0)), pl.BlockSpec((d, _LANES), lambda i: (0, 0)),
                  pl.BlockSpec((_LANES, n), lambda i: (0, 0)), pl.BlockSpec((1, n), lambda i: (0, 0))],
        out_specs=pl.BlockSpec((tm, n), lambda i: (i, 0)),
        out_shape=jax.ShapeDtypeStruct((m, n), F32),
        compiler_params=_params(1),
        name="gla_decay",
    )(h, w1p, w2p, b.reshape(1, n))


def _ml_gates_body(a_ref, w_ref, b_ref, o_ref, *, n_heads):
    z = _softcap(_dot(a_ref[...], w_ref[...].astype(BF16)) + b_ref[...])
    col = lax.broadcasted_iota(jnp.int32, z.shape, 1)
    o_ref[...] = jnp.where(col < n_heads, z, _log_sigmoid(z))


def _ml_gates(h, w_i, b_i, w_f, b_f, tok):
    m, d = h.shape
    nh = w_i.shape[1]
    pad = _LANES - 2 * nh
    w = jnp.pad(jnp.concatenate([w_i, w_f], axis=1), ((0, 0), (0, pad)))
    b = jnp.pad(jnp.concatenate([b_i, b_f]), (0, pad)).reshape(1, _LANES)
    tm = tok.tile
    return pl.pallas_call(
        functools.partial(_ml_gates_body, n_heads=nh),
        grid=(m // tm,),
        in_specs=[pl.BlockSpec((tm, d), lambda i: (i, 0)), pl.BlockSpec((d, _LANES), lambda i: (0, 0)),
                  pl.BlockSpec((1, _LANES), lambda i: (0, 0))],
        out_specs=pl.BlockSpec((tm, _LANES), lambda i: (i, 0)),
        out_shape=jax.ShapeDtypeStruct((m, _LANES), F32),
        compiler_params=_params(1),
        name="ml_gates",
    )(h, w, b)


def _causal(c):
    row = lax.broadcasted_iota(jnp.int32, (c, c), 0)
    col = lax.broadcasted_iota(jnp.int32, (c, c), 1)
    return row, col


def _pad_rows(t, c):
    if t.shape[0] == c:
        return t
    return jnp.concatenate([t, jnp.zeros((c - t.shape[0], t.shape[1]), t.dtype)], axis=0)


def _head_norm(o):
    return o * lax.rsqrt(jnp.mean(o * o, axis=-1, keepdims=True) + EPS)


def _gla_chunk(q, k, v, e_b, e_nb, e_eb, decay_col, s):
    c = q.shape[0]
    row, col = _causal(c)
    q_dec = (q * e_b).astype(BF16)
    k_inv = (k * e_nb).astype(BF16)
    k_end = (k * e_eb).astype(BF16)
    vb = v.astype(BF16)
    att = jnp.where(row >= col, _dot_nt(q_dec, k_inv), 0.0)
    o = _dot(att.astype(BF16), vb) + _dot(q_dec, s.astype(BF16))
    s_new = decay_col * s + _dot_tn(k_end, vb)
    return o, s_new


def _gla_heads(q_ref, k_ref, v_ref, la_ref, r_ref, go_ref, s_in, s_out, write_o, *, n_heads, dk, dv, c):
    row, col = _causal(c)
    tril = (row >= col).astype(BF16)
    ones = jnp.ones((c, _LANES), BF16)
    for hh in range(n_heads):
        sk = slice(hh * dk, (hh + 1) * dk)
        sv = slice(hh * dv, (hh + 1) * dv)
        q = _pad_rows(q_ref[:, sk], c) * dk ** -0.5
        k = _pad_rows(k_ref[:, sk], c)
        v = _pad_rows(v_ref[:, sv], c)
        g = _pad_rows(la_ref[:, sk], c)
        g_hi = g.astype(BF16)
        g_lo = (g - g_hi.astype(F32)).astype(BF16)
        b = _dot(tril, g_hi) + _dot(tril, g_lo)
        b_end = b[c - 1:c, :]
        b_end_col = (_dot_tn(g_hi, ones) + _dot_tn(g_lo, ones))[:, 0:1]
        o, s_new = _gla_chunk(q, k, v, jnp.exp(b), jnp.exp(-b), jnp.exp(b_end - b), jnp.exp(b_end_col), s_in[hh])
        s_out[hh] = s_new.astype(s_out.dtype)
        y = _head_norm(o) * go_ref[...]
        write_o(sv, y * _pad_rows(r_ref[:, sv], c))


def _gla_prompt_body(q_ref, k_ref, v_ref, la_ref, r_ref, go_ref, og_ref, s_ref, *, n_heads, dk, dv):
    @pl.when(pl.program_id(2) == 0)
    def _():
        s_ref[...] = jnp.zeros_like(s_ref)

    def write_o(sv, val):
        og_ref[:, sv] = val.astype(og_ref.dtype)

    _gla_heads(q_ref, k_ref, v_ref, la_ref, r_ref, go_ref, s_ref, s_ref, write_o,
               n_heads=n_heads, dk=dk, dv=dv, c=q_ref.shape[0])


def _gla_sample_body(q_ref, k_ref, v_ref, la_ref, r_ref, go_ref, s0_ref, *rest, n_heads, dk, dv, aliased):
    og_ref, s_ref = rest[-2], rest[-1]
    ls = q_ref.shape[0]

    def write_o(sv, val):
        og_ref[:, sv] = val[:ls].astype(og_ref.dtype)

    _gla_heads(q_ref, k_ref, v_ref, la_ref, r_ref, go_ref, s0_ref, s_ref, write_o,
               n_heads=n_heads, dk=dk, dv=dv, c=_SAMPLE_PAD)


def _ret_heads(q_ref, k_ref, v_ref, g_ref, cos_ref, sin_ref, lg_ref, head0, s_in, s_out, write_o, *,
               n_heads, dk, dv, c, n_valid):
    half = dk // 2
    cos = _pad_rows(cos_ref[...], c)
    sin = _pad_rows(sin_ref[...], c)
    steps = jnp.minimum(lax.broadcasted_iota(jnp.int32, (c, 1), 0) + 1, n_valid).astype(F32)

    def rot(t):
        t1, t2 = t[:, :half], t[:, half:]
        return jnp.concatenate([t1 * cos - t2 * sin, t1 * sin + t2 * cos], axis=-1)

    for hh in range(n_heads):
        sk = slice(hh * dk, (hh + 1) * dk)
        sv = slice(hh * dv, (hh + 1) * dv)
        lg = lg_ref[head0 + hh]
        b = steps * lg
        b_end = jnp.full((1, 1), n_valid, F32) * lg
        q = rot(_pad_rows(q_ref[:, sk], c))
        k = rot(_pad_rows(k_ref[:, sk], c)) * dk ** -0.5
        v = _pad_rows(v_ref[:, sv], c)
        o, s_new = _gla_chunk(q, k, v, jnp.exp(b), jnp.exp(-b), jnp.exp(b_end - b), jnp.exp(b_end), s_in[hh])
        s_out[hh] = s_new.astype(s_out.dtype)
        write_o(sv, _pad_rows(g_ref[:, sv], c) * _head_norm(o))


def _ret_prompt_body(lg_ref, q_ref, k_ref, v_ref, g_ref, cos_ref, sin_ref, og_ref, s_ref, *, n_heads, dk, dv):
    @pl.when(pl.program_id(2) == 0)
    def _():
        s_ref[...] = jnp.zeros_like(s_ref)

    def write_o(sv, val):
        og_ref[:, sv] = val.astype(og_ref.dtype)

    c = q_ref.shape[0]
    _ret_heads(q_ref, k_ref, v_ref, g_ref, cos_ref, sin_ref, lg_ref, pl.program_id(1) * n_heads, s_ref, s_ref,
               write_o, n_heads=n_heads, dk=dk, dv=dv, c=c, n_valid=c)


def _ret_sample_body(lg_ref, q_ref, k_ref, v_ref, g_ref, cos_ref, sin_ref, s0_ref, og_ref, s_ref, *,
                     n_heads, dk, dv):
    ls = q_ref.shape[0]

    def write_o(sv, val):
        og_ref[:, sv] = val[:ls].astype(og_ref.dtype)

    _ret_heads(q_ref, k_ref, v_ref, g_ref, cos_ref, sin_ref, lg_ref, pl.program_id(1) * n_heads, s0_ref, s_ref,
               write_o, n_heads=n_heads, dk=dk, dv=dv, c=_SAMPLE_PAD, n_valid=ls)


def _ml_heads(q_ref, k_ref, v_ref, og_ref, i_ref, f_ref, go_ref, st_in, st_out, write_o, *, n_heads, dk, dv, c):
    c_in, n_in, m_in = st_in
    c_out, n_out, m_out = st_out
    row, col = _causal(c)
    causal = row >= col
    eye = row == col
    neg_inf = jnp.float32(-jnp.inf)
    for hh in range(n_heads):
        sk = slice(hh * dk, (hh + 1) * dk)
        sv = slice(hh * dv, (hh + 1) * dv)
        i_row = i_ref[hh:hh + 1, :]
        f_row = f_ref[hh:hh + 1, :]
        i_col = jnp.sum(jnp.where(eye, i_row, 0.0), axis=1, keepdims=True)
        f_col = jnp.sum(jnp.where(eye, f_row, 0.0), axis=1, keepdims=True)
        fcum_col = jnp.sum(jnp.where(causal, f_row, 0.0), axis=1, keepdims=True)
        fcum_row = jnp.sum(jnp.where(row <= col, f_col, 0.0), axis=0, keepdims=True)
        m_prev = m_in[hh:hh + 1, :]
        logw = jnp.where(causal, fcum_col - fcum_row + i_row, neg_inf)
        log_inter = fcum_col + m_prev
        m_t = jnp.maximum(log_inter, jnp.max(logw, axis=1, keepdims=True))
        w = jnp.exp(logw - m_t)
        a = jnp.exp(log_inter - m_t)
        q = _pad_rows(q_ref[:, sk], c)
        k = _pad_rows(k_ref[:, sk], c) * dk ** -0.5
        qb = q.astype(BF16)
        vb = _pad_rows(v_ref[:, sv], c).astype(BF16)
        s = _dot_nt(qb, k.astype(BF16)) * w
        cm = c_in[hh]
        n = n_in[hh:hh + 1, :]
        num = _dot(s.astype(BF16), vb) + a * _dot(qb, cm.astype(BF16))
        den = jnp.sum(s, axis=1, keepdims=True) + a * jnp.sum(q * n, axis=1, keepdims=True)
        hc = num / jnp.maximum(jnp.abs(den), jnp.exp(-m_t))
        m_new = m_t[c - 1:c, :]
        f_last = fcum_col[c - 1:c, :]
        a_end = jnp.exp(f_last + m_prev - m_new)
        w_end = jnp.exp(f_last - fcum_col + i_col - m_new)
        kw = w_end * k
        c_out[hh] = a_end * cm + _dot_tn(kw.astype(BF16), vb)
        n_out[hh:hh + 1, :] = a_end * n + jnp.sum(kw, axis=0, keepdims=True)
        m_out[hh:hh + 1, :] = m_new
        y = _head_norm(hc) * go_ref[...]
        write_o(sv, _pad_rows(og_ref[:, sv], c) * y)


def _ml_prompt_body(q_ref, k_ref, v_ref, og_ref, i_ref, f_ref, go_ref, out_ref, c_ref, n_ref, m_ref, *,
                    n_heads, dk, dv):
    @pl.when(pl.program_id(1) == 0)
    def _():
        c_ref[...] = jnp.zeros_like(c_ref)
        n_ref[...] = jnp.zeros_like(n_ref)
        m_ref[...] = jnp.zeros_like(m_ref)

    def write_o(sv, val):
        out_ref[:, sv] = val.astype(out_ref.dtype)

    st = (c_ref, n_ref, m_ref)
    _ml_heads(q_ref, k_ref, v_ref, og_ref, i_ref, f_ref, go_ref, st, st, write_o,
              n_heads=n_heads, dk=dk, dv=dv, c=q_ref.shape[0])


def _ml_sample_body(q_ref, k_ref, v_ref, og_ref, i_ref, f_ref, go_ref, c0_ref, n0_ref, m0_ref,
                    out_ref, c_ref, n_ref, m_ref, *, n_heads, dk, dv):
    ls = q_ref.shape[0]

    def write_o(sv, val):
        out_ref[:, sv] = val[:ls].astype(out_ref.dtype)

    _ml_heads(q_ref, k_ref, v_ref, og_ref, i_ref, f_ref, go_ref, (c0_ref, n0_ref, m0_ref),
              (c_ref, n_ref, m_ref), write_o, n_heads=n_heads, dk=dk, dv=dv, c=_SAMPLE_PAD)


def _sample_view(t, tok):
    return t[tok.mp:].reshape(tok.bs, tok.ls, t.shape[1])


def _gla_mix(q, k, v, la, r, g_o, state, layer, prev_sample_states, tok, heads, group_p, group_s):
    nl, bs, nh, dk, dv = state.shape
    c = CHUNK if tok.lp % CHUNK == 0 else tok.lp
    nc = tok.lp // c
    gp = group_p
    tok_blk = lambda w: pl.BlockSpec((c, gp * w), lambda b, hg, ci: (b * nc + ci, hg))
    og_p, s_p = pl.pallas_call(
        functools.partial(_gla_prompt_body, n_heads=gp, dk=dk, dv=dv),
        grid=(tok.bp, nh // gp, nc),
        in_specs=[tok_blk(dk), tok_blk(dk), tok_blk(dv), tok_blk(dk), tok_blk(dv),
                  pl.BlockSpec((1, dv), lambda b, hg, ci: (0, 0))],
        out_specs=[tok_blk(dv), pl.BlockSpec((None, gp, dk, dv), lambda b, hg, ci: (b, hg, 0, 0))],
        out_shape=(jax.ShapeDtypeStruct((tok.mp, nh * dv), BF16), jax.ShapeDtypeStruct((tok.bp, nh, dk, dv), F32)),
        compiler_params=_params(3),
        name="gla_prompt",
    )(q, k, v, la, r, g_o.reshape(1, dv))

    gs = group_s
    smp_blk = lambda w: pl.BlockSpec((None, tok.ls, gs * w), lambda b, hg: (b, 0, hg))
    st_blk = pl.BlockSpec((None, None, gs, dk, dv), lambda b, hg: (layer, b, hg, 0, 0))
    args = [_sample_view(t, tok) for t in (q, k, v, la, r)] + [g_o.reshape(1, dv), state]
    in_specs = [smp_blk(dk), smp_blk(dk), smp_blk(dv), smp_blk(dk), smp_blk(dv),
                pl.BlockSpec((1, dv), lambda b, hg: (0, 0)), st_blk]
    aliases = {}
    if prev_sample_states is not None:
        args.append(prev_sample_states)
        in_specs.append(pl.BlockSpec(memory_space=pl.ANY))
        aliases = {len(args) - 1: 1}
    og_s, s_s = pl.pallas_call(
        functools.partial(_gla_sample_body, n_heads=gs, dk=dk, dv=dv, aliased=prev_sample_states is not None),
        grid=(bs, nh // gs),
        in_specs=in_specs,
        out_specs=[smp_blk(dv), st_blk],
        out_shape=(jax.ShapeDtypeStruct((bs, tok.ls, nh * dv), F32), jax.ShapeDtypeStruct(state.shape, F32)),
        input_output_aliases=aliases,
        compiler_params=_params(2),
        name="gla_sample",
    )(*args)
    og = jnp.concatenate([og_p, og_s.reshape(tok.ms, nh * dv).astype(BF16)], axis=0)
    return og, s_p, s_s


def _rope_tables(pos, half):
    inv = ROPE_BASE ** (-jnp.arange(half, dtype=F32) / half)
    ang = pos[:, None] * inv[None, :]
    return jnp.cos(ang), jnp.sin(ang)


def _ret_mix(q, k, v, g, state, tok, group_p, group_s):
    _, bs, nh, dk, dv = state.shape
    c = CHUNK if tok.lp % CHUNK == 0 else tok.lp
    nc = tok.lp // c
    half = dk // 2
    log_gamma = jnp.log1p(-jnp.exp2(-5.0 - jnp.arange(nh, dtype=F32)))
    cos_p, sin_p = _rope_tables(0 + jnp.arange(tok.lp, dtype=F32), half)
    cos_s, sin_s = _rope_tables(PAST_LEN + jnp.arange(tok.ls, dtype=F32), half)
    gp = group_p
    tok_blk = lambda w: pl.BlockSpec((c, gp * w), lambda b, hg, ci, lg: (b * nc + ci, hg))
    rope_blk = pl.BlockSpec((c, half), lambda b, hg, ci, lg: (ci, 0))
    og_p, s_p = pl.pallas_call(
        functools.partial(_ret_prompt_body, n_heads=gp, dk=dk, dv=dv),
        grid_spec=pltpu.PrefetchScalarGridSpec(
            num_scalar_prefetch=1,
            grid=(tok.bp, nh // gp, nc),
            in_specs=[tok_blk(dk), tok_blk(dk), tok_blk(dv), tok_blk(dv), rope_blk, rope_blk],
            out_specs=[tok_blk(dv), pl.BlockSpec((None, None, gp, dk, dv), lambda b, hg, ci, lg: (0, b, hg, 0, 0))],
        ),
        out_shape=(jax.ShapeDtypeStruct((tok.mp, nh * dv), BF16), jax.ShapeDtypeStruct((1, tok.bp, nh, dk, dv), F32)),
        compiler_params=_params(3),
        name="ret_prompt",
    )(log_gamma, q, k, v, g, cos_p, sin_p)

    gs = group_s
    smp_blk = lambda w: pl.BlockSpec((None, tok.ls, gs * w), lambda b, hg, lg: (b, 0, hg))
    st_blk = pl.BlockSpec((None, None, gs, dk, dv), lambda b, hg, lg: (0, b, hg, 0, 0))
    rope_s = pl.BlockSpec((tok.ls, half), lambda b, hg, lg: (0, 0))
    og_s, s_s = pl.pallas_call(
        functools.partial(_ret_sample_body, n_heads=gs, dk=dk, dv=dv),
        grid_spec=pltpu.PrefetchScalarGridSpec(
            num_scalar_prefetch=1,
            grid=(bs, nh // gs),
            in_specs=[smp_blk(dk), smp_blk(dk), smp_blk(dv), smp_blk(dv), rope_s, rope_s, st_blk],
            out_specs=[smp_blk(dv), st_blk],
        ),
        out_shape=(jax.ShapeDtypeStruct((bs, tok.ls, nh * dv), F32), jax.ShapeDtypeStruct(state.shape, F32)),
        compiler_params=_params(2),
        name="ret_sample",
    )(log_gamma, *[_sample_view(t, tok) for t in (q, k, v, g)], cos_s, sin_s, state)
    og = jnp.concatenate([og_p, og_s.reshape(tok.ms, nh * dv).astype(BF16)], axis=0)
    return og, s_p, s_s


def _ml_mix(q, k, v, og, gates, g_o, c0, n0, m0, tok):
    _, bs, nh, dk, dv = c0.shape
    c = CHUNK if tok.lp % CHUNK == 0 else tok.lp
    nc = tok.lp // c
    g_p = gates[:tok.mp, :2 * nh].reshape(tok.bp, nc, c, 2 * nh).transpose(0, 1, 3, 2)
    g_s = gates[tok.mp:, :2 * nh].reshape(bs, tok.ls, 2 * nh).transpose(0, 2, 1)
    pad = _SAMPLE_PAD - tok.ls
    i_s = jnp.pad(g_s[:, :nh], ((0, 0), (0, 0), (0, pad)), constant_values=-jnp.inf)
    f_s = jnp.pad(g_s[:, nh:], ((0, 0), (0, 0), (0, pad)))
    go = g_o.reshape(1, dv)

    tok_blk = lambda w: pl.BlockSpec((c, nh * w), lambda b, ci: (b * nc + ci, 0))
    gate_blk = lambda first: pl.BlockSpec((None, None, nh, c), lambda b, ci: (b, ci, first, 0))
    out_p, c_p, n_p, m_p = pl.pallas_call(
        functools.partial(_ml_prompt_body, n_heads=nh, dk=dk, dv=dv),
        grid=(tok.bp, nc),
        in_specs=[tok_blk(dk), tok_blk(dk), tok_blk(dv), tok_blk(dv), gate_blk(0), gate_blk(1),
                  pl.BlockSpec((1, dv), lambda b, ci: (0, 0))],
        out_specs=[tok_blk(dv),
                   pl.BlockSpec((None, None, nh, dk, dv), lambda b, ci: (0, b, 0, 0, 0)),
                   pl.BlockSpec((None, None, nh, dk), lambda b, ci: (0, b, 0, 0)),
                   pl.BlockSpec((None, None, nh, 1), lambda b, ci: (0, b, 0, 0))],
        out_shape=(jax.ShapeDtypeStruct((tok.mp, nh * dv), BF16),
                   jax.ShapeDtypeStruct((1, tok.bp, nh, dk, dv), F32),
                   jax.ShapeDtypeStruct((1, tok.bp, nh, dk), F32),
                   jax.ShapeDtypeStruct((1, tok.bp, nh, 1), F32)),
        compiler_params=_params(2),
        name="ml_prompt",
    )(q, k, v, og, g_p, g_p, go)

    smp_blk = lambda w: pl.BlockSpec((None, tok.ls, nh * w), lambda b: (b, 0, 0))
    sgate_blk = pl.BlockSpec((None, nh, _SAMPLE_PAD), lambda b: (b, 0, 0))
    c_blk = pl.BlockSpec((None, None, nh, dk, dv), lambda b: (0, b, 0, 0, 0))
    n_blk = pl.BlockSpec((None, None, nh, dk), lambda b: (0, b, 0, 0))
    m_blk = pl.BlockSpec((None, None, nh, 1), lambda b: (0, b, 0, 0))
    out_s, c_s, n_s, m_s = pl.pallas_call(
        functools.partial(_ml_sample_body, n_heads=nh, dk=dk, dv=dv),
        grid=(bs,),
        in_specs=[smp_blk(dk), smp_blk(dk), smp_blk(dv), smp_blk(dv), sgate_blk, sgate_blk,
                  pl.BlockSpec((1, dv), lambda b: (0, 0)), c_blk, n_blk, m_blk],
        out_specs=[smp_blk(dv), c_blk, n_blk, m_blk],
        out_shape=(jax.ShapeDtypeStruct((bs, tok.ls, nh * dv), F32),
                   jax.ShapeDtypeStruct(c0.shape, F32), jax.ShapeDtypeStruct(n0.shape, F32),
                   jax.ShapeDtypeStruct(m0.shape + (1,), F32)),
        compiler_params=_params(1),
        name="ml_sample",
    )(*[_sample_view(t, tok) for t in (q, k, v, og)], i_s, f_s, go, c0, n0, m0[..., None])
    out = jnp.concatenate([out_p, out_s.reshape(tok.ms, nh * dv).astype(BF16)], axis=0)
    return out, (c_p, n_p, m_p[..., 0]), (c_s, n_s, m_s[..., 0])


def _tiles(bp, lp, bs, ls):
    ms = bs * ls
    mm_tile = min(512, ms)
    norm_tile = min(256, ms)
    return _Tok(bp, lp, bs, ls, mm_tile), _Tok(bp, lp, bs, ls, norm_tile)


def kernel(x_prompt, x_sample, state_gla, state_ret, state_mlstm_C, state_mlstm_n, state_mlstm_m, c_prompt, c_sample, w_ada, b_ada, g_mix, g_mlp, w_up, w_down, g_final, w_gla_q, w_gla_k, w_gla_v, w_gla_a1, w_gla_a2, b_gla_a, w_gla_r, b_gla_r, g_gla_o, w_gla_o, w_ret_q, w_ret_k, w_ret_v, w_ret_g, w_ret_o, w_ml_q, w_ml_k, w_ml_v, w_ml_i, b_ml_i, w_ml_f, b_ml_f, w_ml_og, g_ml_o, w_ml_o):
    bp, lp, d = x_prompt.shape
    bs, ls, _ = x_sample.shape
    depth = w_ada.shape[0]
    tok, ntok = _tiles(bp, lp, bs, ls)
    tn = min(1024, d)

    x = jnp.concatenate([x_prompt.reshape(bp * lp, d), x_sample.reshape(bs * ls, d)], axis=0)

    n_c = bp + bs
    rows = -(-n_c // 16) * 16
    c_all = jnp.pad(jnp.concatenate([c_prompt, c_sample], axis=0), ((0, rows - n_c), (0, 0)))
    mod = _ada_mod(c_all, w_ada, b_ada, min(512, d))

    def mod_pair(layer, idx):
        chunk = mod[layer, :, idx * d:(idx + 1) * d]
        return chunk[:bp].reshape(bp, 1, d), jnp.repeat(chunk[bp:n_c], ls, axis=0)

    new_gla_p, gla_s = [], None
    ret_p = ret_s = ml_p = ml_s = None
    for i in range(depth):
        kind, j = i % 3, i // 3
        sh1, sc1, gt1, sh2, sc2, gt2 = (mod_pair(i, t) for t in range(6))
        h = _norm_mod(x, g_mix[i], sc1, sh1, ntok)
        if kind == 0:
            q = _mm(h, w_gla_q, j, tok, tn=tn, name="gla_q")
            k = _mm(h, w_gla_k, j, tok, tn=tn, name="gla_k")
            v = _mm(h, w_gla_v, j, tok, tn=tn, name="gla_v")
            r = _mm(h, w_gla_r, j, tok, tn=tn, epi=_epi_bias_silu, extras=_bias_extras(b_gla_r[j], tn), name="gla_r")
            la = _gla_decay(h, w_gla_a1[j], w_gla_a2[j], b_gla_a[j], tok)
            og, s_p, gla_s = _gla_mix(q, k, v, la, r, g_gla_o[j], state_gla, j, gla_s, tok,
                                      state_gla.shape[2], min(4, state_gla.shape[2]), min(2, state_gla.shape[2]))
            new_gla_p.append(s_p)
            w_o = w_gla_o
        elif kind == 1:
            q = _mm(h, w_ret_q, j, tok, tn=tn, name="ret_q")
            k = _mm(h, w_ret_k, j, tok, tn=tn, name="ret_k")
            v = _mm(h, w_ret_v, j, tok, tn=tn, name="ret_v")
            g = _mm(h, w_ret_g, j, tok, tn=tn, epi=_epi_silu, name="ret_g")
            nh = state_ret.shape[2]
            og, ret_p, ret_s = _ret_mix(q, k, v, g, state_ret, tok, min(4, nh), min(8, nh))
            w_o = w_ret_o
        else:
            q = _mm(h, w_ml_q, j, tok, tn=tn, name="ml_q")
            k = _mm(h, w_ml_k, j, tok, tn=tn, name="ml_k")
            v = _mm(h, w_ml_v, j, tok, tn=tn, name="ml_v")
            ogate = _mm(h, w_ml_og, j, tok, tn=tn, epi=_epi_sigmoid, name="ml_og")
            gates = _ml_gates(h, w_ml_i[j], b_ml_i[j], w_ml_f[j], b_ml_f[j], tok)
            og, ml_p, ml_s = _ml_mix(q, k, v, ogate, gates, g_ml_o[j], state_mlstm_C, state_mlstm_n,
                                     state_mlstm_m, tok)
            w_o = w_ml_o
        x = _mm_resid(og, w_o, j, x, gt1, tok, d, name="mix_o")
        hf = _norm_mod(x, g_mlp[i], sc2, sh2, ntok)
        hid = _mm(hf, w_up, i, tok, tn=tn, epi=_epi_relu2, out_dtype=BF16, name="mlp_up")
        x = _mm_resid(hid, w_down, i, x, gt2, tok, d, name="mlp_down")

    y_p = _final_norm(x, g_final, ntok.tile, 0, ntok.npt).reshape(bp, lp, d)
    y_s = _final_norm(x, g_final, ntok.tile, ntok.npt, ntok.nst).reshape(bs, ls, d)
    return (y_p, y_s, jnp.stack(new_gla_p), gla_s, ret_p, ret_s,
            ml_p[0], ml_s[0], ml_p[1], ml_s[1], ml_p[2], ml_s[2])
```

```python
import functools

import jax
import jax.numpy as jnp
from jax import lax
from jax.experimental import pallas as pl
from jax.experimental.pallas import tpu as pltpu

F32 = jnp.float32
BF16 = jnp.bfloat16

CHUNK = 64
EPS = 1e-6
GLA_TAU = 16.0
ML_GATE_CAP = 15.0
ROPE_BASE = 10000.0
PAST_LEN = 16384

_VMEM_LIMIT_V7X = 60 * 2**20
_LANES = 128
_SUBLANES = 8
_MM_TILE = 512
_NORM_TILE = 256
_MM_TN = 1024
_RESID_TN = 512
_ADA_TN = 512


def _params(n_grid, limit=_VMEM_LIMIT_V7X):
    return pltpu.CompilerParams(dimension_semantics=("arbitrary",) * n_grid, vmem_limit_bytes=limit)


def _dot(a, b):
    return jnp.dot(a, b, preferred_element_type=F32)


def _dot_nt(a, b):
    return lax.dot_general(a, b, (((1,), (1,)), ((), ())), preferred_element_type=F32)


def _dot_tn(a, b):
    return lax.dot_general(a, b, (((0,), (0,)), ((), ())), preferred_element_type=F32)


def _sigmoid(x):
    return 1.0 / (1.0 + jnp.exp(-x))


def _silu(x):
    return x * _sigmoid(x)


def _log_sigmoid(x):
    return jnp.minimum(x, 0.0) - jnp.log1p(jnp.exp(-jnp.abs(x)))


def _softcap(z):
    return ML_GATE_CAP * jnp.tanh(z / ML_GATE_CAP)


class _Tok:
    def __init__(self, bp, lp, bs, ls, tile):
        self.bp, self.lp, self.bs, self.ls = bp, lp, bs, ls
        self.mp, self.ms = bp * lp, bs * ls
        self.m = self.mp + self.ms
        assert lp % tile == 0 and self.ms % tile == 0, (lp, self.ms, tile)
        self.tile = tile
        self.npt = self.mp // tile
        self.nst = self.ms // tile
        self.tpb = lp // tile

    def prompt_idx(self, i):
        return jnp.minimum(i // self.tpb, self.bp - 1)

    def sample_idx(self, i):
        return jnp.maximum(i - self.npt, 0)


class _Mod:
    def __init__(self, mod, bp, bs, ls, d):
        depth = mod.shape[0]
        self.p = mod[:, :bp].reshape(depth, bp, 1, mod.shape[2])
        self.s = jnp.repeat(mod[:, bp:bp + bs], ls, axis=1)
        self.d = d

    def specs(self, layer, chunk, tok, width, col):
        per = self.d // width
        return [
            (self.p, (None, None, 1, width), lambda i, j: (layer, tok.prompt_idx(i), 0, chunk * per + col(j))),
            (self.s, (None, tok.tile, width), lambda i, j: (layer, tok.sample_idx(i), chunk * per + col(j))),
        ]


def _ada_body(c_ref, w_ref, b_ref, o_ref):
    sc = _silu(c_ref[...]).astype(BF16)
    o_ref[...] = _dot(sc, w_ref[...].astype(BF16)) + b_ref[...]


def _ada_mod(c_all, w_ada, b_ada, tn):
    depth, d, n = w_ada.shape
    rows = c_all.shape[0]
    return pl.pallas_call(
        _ada_body,
        grid=(depth, n // tn),
        in_specs=[
            pl.BlockSpec((rows, d), lambda l, j: (0, 0)),
            pl.BlockSpec((None, d, tn), lambda l, j: (l, 0, j)),
            pl.BlockSpec((None, 1, tn), lambda l, j: (l, 0, j)),
        ],
        out_specs=pl.BlockSpec((None, rows, tn), lambda l, j: (l, 0, j)),
        out_shape=jax.ShapeDtypeStruct((depth, rows, n), F32),
        compiler_params=_params(2),
        name="ada_mod",
    )(c_all, w_ada, b_ada.reshape(depth, 1, n))


def _rms(x, g):
    return x * lax.rsqrt(jnp.mean(x * x, axis=-1, keepdims=True) + EPS) * g


def _norm_mod_body(x_ref, g_ref, scp_ref, scs_ref, shp_ref, shs_ref, o_ref, *, npt):
    i = pl.program_id(0)
    y = _rms(x_ref[...], g_ref[...])

    @pl.when(i < npt)
    def _():
        o_ref[...] = (y * (1.0 + scp_ref[...]) + shp_ref[...]).astype(o_ref.dtype)

    @pl.when(i >= npt)
    def _():
        o_ref[...] = (y * (1.0 + scs_ref[...]) + shs_ref[...]).astype(o_ref.dtype)


def _norm_mod(x, g, mod, layer, scale_chunk, shift_chunk, tok):
    m, d = x.shape
    t = tok.tile
    ex = mod.specs(layer, scale_chunk, tok, d, lambda j: 0) + mod.specs(layer, shift_chunk, tok, d, lambda j: 0)
    return pl.pallas_call(
        functools.partial(_norm_mod_body, npt=tok.npt),
        grid=(m // t,),
        in_specs=[pl.BlockSpec((t, d), lambda i: (i, 0)), pl.BlockSpec((1, d), lambda i: (0, 0))]
        + [pl.BlockSpec(blk, functools.partial(lambda f, i: f(i, 0), f)) for _, blk, f in ex],
        out_specs=pl.BlockSpec((t, d), lambda i: (i, 0)),
        out_shape=jax.ShapeDtypeStruct((m, d), BF16),
        compiler_params=_params(1),
        name="norm_mod",
    )(x, g.reshape(1, d), *[e[0] for e in ex])


def _final_norm_body(x_ref, g_ref, o_ref):
    o_ref[...] = _rms(x_ref[...], g_ref[...])


def _final_norm(x, g, tile, first_tile, n_tiles):
    d = x.shape[1]
    return pl.pallas_call(
        _final_norm_body,
        grid=(n_tiles,),
        in_specs=[pl.BlockSpec((tile, d), lambda i: (i + first_tile, 0)), pl.BlockSpec((1, d), lambda i: (0, 0))],
        out_specs=pl.BlockSpec((tile, d), lambda i: (i, 0)),
        out_shape=jax.ShapeDtypeStruct((n_tiles * tile, d), F32),
        compiler_params=_params(1),
        name="final_norm",
    )(x, g.reshape(1, d))


def _epi_plain(acc, ex, is_prompt):
    return acc


def _epi_bias_silu(acc, ex, is_prompt):
    return _silu(acc + ex[0][...])


def _epi_silu(acc, ex, is_prompt):
    return _silu(acc)


def _epi_sigmoid(acc, ex, is_prompt):
    return _sigmoid(acc)


def _epi_relu2(acc, ex, is_prompt):
    r = jnp.maximum(acc, 0.0)
    return r * r


def _epi_add(acc, ex, is_prompt):
    return ex[0][...] + acc


def _epi_resid(acc, ex, is_prompt):
    x_ref, gp_ref, gs_ref = ex[:3]
    if len(ex) == 4:
        acc = ex[3][...] + acc
    gate = jnp.where(is_prompt, gp_ref[...], gs_ref[...])
    return x_ref[...] + gate * acc


def _mm_body(*refs, epi, n_act, n_ex, npt):
    acts, w_ref = refs[:n_act], refs[n_act]
    ex = refs[n_act + 1:n_act + 1 + n_ex]
    o_ref, wbf_ref = refs[n_act + 1 + n_ex], refs[n_act + 2 + n_ex]
    i = pl.program_id(1)

    @pl.when(i == 0)
    def _():
        wbf_ref[...] = w_ref[...].astype(BF16)

    def run(a_ref):
        acc = _dot(a_ref[...], wbf_ref[...])
        o_ref[...] = epi(acc, ex, i < npt).astype(o_ref.dtype)

    if n_act == 1:
        run(acts[0])
    else:
        pl.when(i < npt)(lambda: run(acts[0]))
        pl.when(i >= npt)(lambda: run(acts[1]))


def _mm(a, w, layer, tok, *, tn, tk=None, kb=0, epi=_epi_plain, extras=(), out_dtype=F32, name="mm"):
    acts = a if isinstance(a, tuple) else (a,)
    kdim = acts[0].shape[1]
    n = w.shape[2]
    tm = tok.tile
    tn = min(tn, n)
    tk = kdim if tk is None else tk
    assert n % tn == 0 and kdim % tk == 0 and w.shape[1] == kdim
    if len(acts) == 1:
        a_specs = [pl.BlockSpec((tm, tk), lambda j, i: (i, kb))]
    else:
        a_specs = [pl.BlockSpec((tm, tk), lambda j, i: (jnp.minimum(i, tok.npt - 1), kb)),
                   pl.BlockSpec((tm, tk), lambda j, i: (tok.sample_idx(i), kb))]
    ex_specs = [pl.BlockSpec(blk, functools.partial(lambda f, j, i: f(i, j), f)) for _, blk, f in extras]
    return pl.pallas_call(
        functools.partial(_mm_body, epi=epi, n_act=len(acts), n_ex=len(extras), npt=tok.npt),
        grid=(n // tn, tok.m // tm),
        in_specs=a_specs + [pl.BlockSpec((None, tk, tn), lambda j, i: (layer, kb, j))] + ex_specs,
        out_specs=pl.BlockSpec((tm, tn), lambda j, i: (i, j)),
        out_shape=jax.ShapeDtypeStruct((tok.m, n), out_dtype),
        scratch_shapes=[pltpu.VMEM((tk, tn), BF16)],
        compiler_params=_params(2),
        name=name,
    )(*acts, w, *[e[0] for e in extras])


def _tile_extra(t, tok, tn):
    return (t, (tok.tile, tn), lambda i, j: (i, j))


def _mm_resid(a, w, layer, x, mod, mod_layer, gate_chunk, tok, d, name):
    acts = a if isinstance(a, tuple) else (a,)
    kdim = acts[0].shape[1]
    nk = -(-kdim // d)
    tk = kdim // nk
    part = None
    for kb in range(nk - 1):
        tn = _MM_TN if len(acts) == 1 else _RESID_TN
        tn = min(tn, w.shape[2])
        ex = [] if part is None else [_tile_extra(part, tok, tn)]
        part = _mm(a, w, layer, tok, tn=tn, tk=tk, kb=kb, epi=_epi_plain if part is None else _epi_add, extras=ex,
                   name=name + "_part")
    tn = min(_RESID_TN, w.shape[2])
    ex = [_tile_extra(x, tok, tn)] + mod.specs(mod_layer, gate_chunk, tok, tn, lambda j: j)
    if part is not None:
        ex.append(_tile_extra(part, tok, tn))
    return _mm(a, w, layer, tok, tn=tn, tk=tk, kb=nk - 1, epi=_epi_resid, extras=ex, name=name)


def _bias_extras(b, tn):
    return [(b.reshape(1, -1), (1, tn), lambda i, j: (0, j))]


def _gla_decay_body(a_ref, w1_ref, w2_ref, b_ref, o_ref):
    t = _dot(a_ref[...], w1_ref[...].astype(BF16)).astype(BF16)
    z = _dot(t, w2_ref[...].astype(BF16)) + b_ref[...]
    o_ref[...] = _log_sigmoid(z) / GLA_TAU


def _gla_decay(h, w1, w2, b, tok):
    m, d = h.shape
    rank, n = w2.shape
    pad = _LANES - rank
    w1p = jnp.pad(w1, ((0, 0), (0, pad)))
    w2p = jnp.pad(w2, ((0, pad), (0, 0)))
    tm = tok.tile
    return pl.pallas_call(
        _gla_decay_body,
        grid=(m // tm,),
        in_specs=[pl.BlockSpec((tm, d), lambda i: (i, 0)), pl.BlockSpec((d, _LANES), lambda i: (0, 0)),
                  pl.BlockSpec((_LANES, n), lambda i: (0, 0)), pl.BlockSpec((1, n), lambda i: (0, 0))],
        out_specs=pl.BlockSpec((tm, n), lambda i: (i, 0)),
        out_shape=jax.ShapeDtypeStruct((m, n), F32),
        compiler_params=_params(1),
        name="gla_decay",
    )(h, w1p, w2p, b.reshape(1, n))


def _ml_gates_body(a_ref, w_ref, b_ref, o_ref, *, n_heads):
    z = _softcap(_dot(a_ref[...], w_ref[...].astype(BF16)) + b_ref[...])
    col = lax.broadcasted_iota(jnp.int32, z.shape, 1)
    o_ref[...] = jnp.where(col < n_heads, z, _log_sigmoid(z))


def _ml_gates(h, w_i, b_i, w_f, b_f, tok):
    m, d = h.shape
    nh = w_i.shape[1]
    pad = _LANES - 2 * nh
    w = jnp.pad(jnp.concatenate([w_i, w_f], axis=1), ((0, 0), (0, pad)))
    b = jnp.pad(jnp.concatenate([b_i, b_f]), (0, pad)).reshape(1, _LANES)
    tm = tok.tile
    return pl.pallas_call(
        functools.partial(_ml_gates_body, n_heads=nh),
        grid=(m // tm,),
        in_specs=[pl.BlockSpec((tm, d), lambda i: (i, 0)), pl.BlockSpec((d, _LANES), lambda i: (0, 0)),
                  pl.BlockSpec((1, _LANES), lambda i: (0, 0))],
        out_specs=pl.BlockSpec((tm, _LANES), lambda i: (i, 0)),
        out_shape=jax.ShapeDtypeStruct((m, _LANES), F32),
        compiler_params=_params(1),
        name="ml_gates",
    )(h, w, b)


def _causal(c):
    row = lax.broadcasted_iota(jnp.int32, (c, c), 0)
    col = lax.broadcasted_iota(jnp.int32, (c, c), 1)
    return row, col


def _seq_masks(c, ls):
    row = lax.broadcasted_iota(jnp.int32, (c, 1), 0)
    return [jnp.logical_and(row >= s * ls, row < (s + 1) * ls) for s in range(c // ls)]


def _head_norm(o):
    return o * lax.rsqrt(jnp.mean(o * o, axis=-1, keepdims=True) + EPS)


def _gla_chunk(q, k, v, e_b, e_nb, e_eb, decay_col, s):
    c = q.shape[0]
    row, col = _causal(c)
    q_dec = (q * e_b).astype(BF16)
    k_inv = (k * e_nb).astype(BF16)
    k_end = (k * e_eb).astype(BF16)
    vb = v.astype(BF16)
    att = jnp.where(row >= col, _dot_nt(q_dec, k_inv), 0.0)
    o = _dot(att.astype(BF16), vb) + _dot(q_dec, s.astype(BF16))
    s_new = decay_col * s + _dot_tn(k_end, vb)
    return o, s_new


def _gla_one(q, k, v, g, s):
    c = q.shape[0]
    row, col = _causal(c)
    tril = (row >= col).astype(BF16)
    g_hi = g.astype(BF16)
    g_lo = (g - g_hi.astype(F32)).astype(BF16)
    b = _dot(tril, g_hi) + _dot(tril, g_lo)
    b_end = b[c - 1:c, :]
    b_end_col = jnp.transpose(b[c - _SUBLANES:c, :])[:, _SUBLANES - 1:_SUBLANES]
    return _gla_chunk(q, k, v, jnp.exp(b), jnp.exp(-b), jnp.exp(b_end - b), jnp.exp(b_end_col), s)


def _gla_prompt_body(q_ref, k_ref, v_ref, la_ref, r_ref, go_ref, og_ref, s_ref, *, n_heads, dk, dv):
    @pl.when(pl.program_id(2) == 0)
    def _():
        s_ref[...] = jnp.zeros_like(s_ref)

    for hh in range(n_heads):
        sk = slice(hh * dk, (hh + 1) * dk)
        sv = slice(hh * dv, (hh + 1) * dv)
        o, s_new = _gla_one(q_ref[:, sk] * dk ** -0.5, k_ref[:, sk], v_ref[:, sv], la_ref[:, sk], s_ref[hh])
        s_ref[hh] = s_new
        og_ref[:, sv] = (_head_norm(o) * go_ref[...] * r_ref[:, sv]).astype(og_ref.dtype)


def _gla_sample_body(q_ref, k_ref, v_ref, la_ref, r_ref, go_ref, s0_ref, *rest, n_heads, dk, dv, ls):
    og_ref, s_ref = rest[-2], rest[-1]
    masks = _seq_masks(q_ref.shape[0], ls)
    for hh in range(n_heads):
        sk = slice(hh * dk, (hh + 1) * dk)
        sv = slice(hh * dv, (hh + 1) * dv)
        out = jnp.zeros((q_ref.shape[0], dv), F32)
        for bb, m in enumerate(masks):
            z = lambda t: jnp.where(m, t, 0.0)
            o, s_new = _gla_one(z(q_ref[:, sk]) * dk ** -0.5, z(k_ref[:, sk]), z(v_ref[:, sv]), z(la_ref[:, sk]),
                                s0_ref[bb, hh])
            s_ref[bb, hh] = s_new
            out = jnp.where(m, _head_norm(o) * go_ref[...], out)
        og_ref[:, sv] = out * r_ref[:, sv]


def _ret_one(q, k, v, steps, n_valid, lg, cos, sin, s):
    dk = q.shape[1]
    half = dk // 2

    def rot(t):
        t1, t2 = t[:, :half], t[:, half:]
        return jnp.concatenate([t1 * cos - t2 * sin, t1 * sin + t2 * cos], axis=-1)

    b = steps * lg
    b_end = jnp.full((1, 1), n_valid, F32) * lg
    return _gla_chunk(rot(q), rot(k) * dk ** -0.5, v, jnp.exp(b), jnp.exp(-b), jnp.exp(b_end - b), jnp.exp(b_end), s)


def _ret_prompt_body(lg_ref, q_ref, k_ref, v_ref, g_ref, cos_ref, sin_ref, og_ref, s_ref, *, n_heads, dk, dv):
    @pl.when(pl.program_id(2) == 0)
    def _():
        s_ref[...] = jnp.zeros_like(s_ref)

    c = q_ref.shape[0]
    steps = (lax.broadcasted_iota(jnp.int32, (c, 1), 0) + 1).astype(F32)
    head0 = pl.program_id(1) * n_heads
    for hh in range(n_heads):
        sk = slice(hh * dk, (hh + 1) * dk)
        sv = slice(hh * dv, (hh + 1) * dv)
        o, s_new = _ret_one(q_ref[:, sk], k_ref[:, sk], v_ref[:, sv], steps, c, lg_ref[head0 + hh],
                            cos_ref[...], sin_ref[...], s_ref[hh])
        s_ref[hh] = s_new
        og_ref[:, sv] = (g_ref[:, sv] * _head_norm(o)).astype(og_ref.dtype)


def _ret_sample_body(lg_ref, q_ref, k_ref, v_ref, g_ref, cos_ref, sin_ref, s0_ref, og_ref, s_ref, *,
                     n_heads, dk, dv, ls):
    c = q_ref.shape[0]
    masks = _seq_masks(c, ls)
    row = lax.broadcasted_iota(jnp.int32, (c, 1), 0)
    head0 = pl.program_id(1) * n_heads
    for hh in range(n_heads):
        sk = slice(hh * dk, (hh + 1) * dk)
        sv = slice(hh * dv, (hh + 1) * dv)
        out = jnp.zeros((c, dv), F32)
        for bb, m in enumerate(masks):
            z = lambda t: jnp.where(m, t, 0.0)
            steps = jnp.clip(row - bb * ls + 1, 0, ls).astype(F32)
            o, s_new = _ret_one(z(q_ref[:, sk]), z(k_ref[:, sk]), z(v_ref[:, sv]), steps, ls, lg_ref[head0 + hh],
                                cos_ref[...], sin_ref[...], s0_ref[bb, hh])
            s_ref[bb, hh] = s_new
            out = jnp.where(m, _head_norm(o), out)
        og_ref[:, sv] = g_ref[:, sv] * out


def _ml_one(q, k, v, i_row, f_row, cm, n, m_prev):
    c = q.shape[0]
    row, col = _causal(c)
    causal = row >= col
    eye = row == col
    i_col = jnp.sum(jnp.where(eye, i_row, 0.0), axis=1, keepdims=True)
    f_col = jnp.sum(jnp.where(eye, f_row, 0.0), axis=1, keepdims=True)
    fcum_col = jnp.sum(jnp.where(causal, f_row, 0.0), axis=1, keepdims=True)
    fcum_row = jnp.sum(jnp.where(row <= col, f_col, 0.0), axis=0, keepdims=True)
    logw = jnp.where(causal, fcum_col - fcum_row + i_row, -jnp.inf)
    log_inter = fcum_col + m_prev
    m_t = jnp.maximum(log_inter, jnp.max(logw, axis=1, keepdims=True))
    w = jnp.exp(logw - m_t)
    a = jnp.exp(log_inter - m_t)
    qb = q.astype(BF16)
    vb = v.astype(BF16)
    s = _dot_nt(qb, k.astype(BF16)) * w
    num = _dot(s.astype(BF16), vb) + a * _dot(qb, cm.astype(BF16))
    den = jnp.sum(s, axis=1, keepdims=True) + a * jnp.sum(q * n, axis=1, keepdims=True)
    h = num / jnp.maximum(jnp.abs(den), jnp.exp(-m_t))
    m_new = m_t[c - 1:c, :]
    f_last = fcum_col[c - 1:c, :]
    a_end = jnp.exp(f_last + m_prev - m_new)
    kw = jnp.exp(f_last - fcum_col + i_col - m_new) * k
    c_new = a_end * cm + _dot_tn(kw.astype(BF16), vb)
    n_new = a_end * n + jnp.sum(kw, axis=0, keepdims=True)
    return h, c_new, n_new, m_new


def _ml_prompt_body(q_ref, k_ref, v_ref, og_ref, i_ref, f_ref, go_ref, out_ref, c_ref, n_ref, m_ref, *,
                    n_heads, dk, dv):
    @pl.when(pl.program_id(1) == 0)
    def _():
        c_ref[...] = jnp.zeros_like(c_ref)
        n_ref[...] = jnp.zeros_like(n_ref)
        m_ref[...] = jnp.zeros_like(m_ref)

    for hh in range(n_heads):
        sk = slice(hh * dk, (hh + 1) * dk)
        sv = slice(hh * dv, (hh + 1) * dv)
        h, c_new, n_new, m_new = _ml_one(q_ref[:, sk], k_ref[:, sk] * dk ** -0.5, v_ref[:, sv],
                                         i_ref[hh:hh + 1, :], f_ref[hh:hh + 1, :],
                                         c_ref[hh], n_ref[hh:hh + 1, :], m_ref[hh:hh + 1, :])
        c_ref[hh] = c_new
        n_ref[hh:hh + 1, :] = n_new
        m_ref[hh:hh + 1, :] = m_new
        out_ref[:, sv] = (og_ref[:, sv] * (_head_norm(h) * go_ref[...])).astype(out_ref.dtype)


def _ml_sample_body(q_ref, k_ref, v_ref, og_ref, i_ref, f_ref, go_ref, c0_ref, n0_ref, m0_ref,
                    out_ref, c_ref, n_ref, m_ref, *, n_heads, dk, dv, ls):
    rows = q_ref.shape[0]
    masks = _seq_masks(rows, ls)
    for hh in range(n_heads):
        sk = slice(hh * dk, (hh + 1) * dk)
        sv = slice(hh * dv, (hh + 1) * dv)
        out = jnp.zeros((rows, dv), F32)
        for bb, m in enumerate(masks):
            z = lambda t: jnp.where(m, t, 0.0)
            h, c_new, n_new, m_new = _ml_one(z(q_ref[:, sk]), z(k_ref[:, sk]) * dk ** -0.5, z(v_ref[:, sv]),
                                             i_ref[bb, hh:hh + 1, :], f_ref[bb, hh:hh + 1, :],
                                             c0_ref[bb, hh], n0_ref[bb, hh:hh + 1, :], m0_ref[bb, hh:hh + 1, :])
            c_ref[bb, hh] = c_new
            n_ref[bb, hh:hh + 1, :] = n_new
            m_ref[bb, hh:hh + 1, :] = m_new
            out = jnp.where(m, _head_norm(h) * go_ref[...], out)
        out_ref[:, sv] = og_ref[:, sv] * out


def _chunking(tok):
    c = CHUNK if tok.lp % CHUNK == 0 else tok.lp
    return c, tok.lp // c


def _pair_rows(tok):
    assert _SUBLANES % tok.ls == 0 and tok.mp % _SUBLANES == 0
    nb = _SUBLANES // tok.ls
    assert tok.bs % nb == 0
    return nb, tok.mp // _SUBLANES


def _gla_mix(q, k, v, la, r, g_o, state, layer, prev_sample_states, tok, group_p, group_s):
    nl, bs, nh, dk, dv = state.shape
    c, nc = _chunking(tok)
    gp = group_p
    tok_blk = lambda w: pl.BlockSpec((c, gp * w), lambda b, hg, ci: (b * nc + ci, hg))
    og_p, s_p = pl.pallas_call(
        functools.partial(_gla_prompt_body, n_heads=gp, dk=dk, dv=dv),
        grid=(tok.bp, nh // gp, nc),
        in_specs=[tok_blk(dk), tok_blk(dk), tok_blk(dv), tok_blk(dk), tok_blk(dv),
                  pl.BlockSpec((1, dv), lambda b, hg, ci: (0, 0))],
        out_specs=[tok_blk(dv), pl.BlockSpec((None, gp, dk, dv), lambda b, hg, ci: (b, hg, 0, 0))],
        out_shape=(jax.ShapeDtypeStruct((tok.mp, nh * dv), BF16), jax.ShapeDtypeStruct((tok.bp, nh, dk, dv), F32)),
        compiler_params=_params(3),
        name="gla_prompt",
    )(q, k, v, la, r, g_o.reshape(1, dv))

    gs = group_s
    nb, t0 = _pair_rows(tok)
    smp_blk = lambda w: pl.BlockSpec((_SUBLANES, gs * w), lambda b, hg: (t0 + b, hg))
    st_blk = pl.BlockSpec((None, nb, gs, dk, dv), lambda b, hg: (layer, b, hg, 0, 0))
    args = [q, k, v, la, r, g_o.reshape(1, dv), state]
    in_specs = [smp_blk(dk), smp_blk(dk), smp_blk(dv), smp_blk(dk), smp_blk(dv),
                pl.BlockSpec((1, dv), lambda b, hg: (0, 0)), st_blk]
    aliases = {}
    if prev_sample_states is not None:
        args.append(prev_sample_states)
        in_specs.append(pl.BlockSpec(memory_space=pl.ANY))
        aliases = {len(args) - 1: 1}
    og_s, s_s = pl.pallas_call(
        functools.partial(_gla_sample_body, n_heads=gs, dk=dk, dv=dv, ls=tok.ls),
        grid=(bs // nb, nh // gs),
        in_specs=in_specs,
        out_specs=[pl.BlockSpec((_SUBLANES, gs * dv), lambda b, hg: (b, hg)), st_blk],
        out_shape=(jax.ShapeDtypeStruct((tok.ms, nh * dv), F32), jax.ShapeDtypeStruct(state.shape, F32)),
        input_output_aliases=aliases,
        compiler_params=_params(2),
        name="gla_sample",
    )(*args)
    return (og_p, og_s.astype(BF16)), s_p, s_s


def _rope_tables(pos, half):
    inv = ROPE_BASE ** (-jnp.arange(half, dtype=F32) / half)
    ang = pos[:, None] * inv[None, :]
    return jnp.cos(ang), jnp.sin(ang)


def _ret_mix(q, k, v, g, state, tok, group_p, group_s):
    _, bs, nh, dk, dv = state.shape
    c, nc = _chunking(tok)
    half = dk // 2
    log_gamma = jnp.log1p(-jnp.exp2(-5.0 - jnp.arange(nh, dtype=F32)))
    cos_p, sin_p = _rope_tables(0 + jnp.arange(tok.lp, dtype=F32), half)
    cos_s, sin_s = _rope_tables(PAST_LEN + jnp.arange(tok.ls, dtype=F32), half)
    gp = group_p
    tok_blk = lambda w: pl.BlockSpec((c, gp * w), lambda b, hg, ci, lg: (b * nc + ci, hg))
    rope_blk = pl.BlockSpec((c, half), lambda b, hg, ci, lg: (ci, 0))
    og_p, s_p = pl.pallas_call(
        functools.partial(_ret_prompt_body, n_heads=gp, dk=dk, dv=dv),
        grid_spec=pltpu.PrefetchScalarGridSpec(
            num_scalar_prefetch=1,
            grid=(tok.bp, nh // gp, nc),
            in_specs=[tok_blk(dk), tok_blk(dk), tok_blk(dv), tok_blk(dv), rope_blk, rope_blk],
            out_specs=[tok_blk(dv), pl.BlockSpec((None, None, gp, dk, dv), lambda b, hg, ci, lg: (0, b, hg, 0, 0))],
        ),
        out_shape=(jax.ShapeDtypeStruct((tok.mp, nh * dv), BF16), jax.ShapeDtypeStruct((1, tok.bp, nh, dk, dv), F32)),
        compiler_params=_params(3),
        name="ret_prompt",
    )(log_gamma, q, k, v, g, cos_p, sin_p)

    gs = group_s
    nb, t0 = _pair_rows(tok)
    smp_blk = lambda w: pl.BlockSpec((_SUBLANES, gs * w), lambda b, hg, lg: (t0 + b, hg))
    st_blk = pl.BlockSpec((None, nb, gs, dk, dv), lambda b, hg, lg: (0, b, hg, 0, 0))
    rope_s = pl.BlockSpec((_SUBLANES, half), lambda b, hg, lg: (0, 0))
    og_s, s_s = pl.pallas_call(
        functools.partial(_ret_sample_body, n_heads=gs, dk=dk, dv=dv, ls=tok.ls),
        grid_spec=pltpu.PrefetchScalarGridSpec(
            num_scalar_prefetch=1,
            grid=(bs // nb, nh // gs),
            in_specs=[smp_blk(dk), smp_blk(dk), smp_blk(dv), smp_blk(dv), rope_s, rope_s, st_blk],
            out_specs=[pl.BlockSpec((_SUBLANES, gs * dv), lambda b, hg, lg: (b, hg)), st_blk],
        ),
        out_shape=(jax.ShapeDtypeStruct((tok.ms, nh * dv), F32), jax.ShapeDtypeStruct(state.shape, F32)),
        compiler_params=_params(2),
        name="ret_sample",
    )(log_gamma, q, k, v, g, jnp.tile(cos_s, (nb, 1)), jnp.tile(sin_s, (nb, 1)), state)
    return (og_p, og_s.astype(BF16)), s_p, s_s


def _ml_mix(q, k, v, og, gates, g_o, c0, n0, m0, tok):
    _, bs, nh, dk, dv = c0.shape
    c, nc = _chunking(tok)
    nb, t0 = _pair_rows(tok)
    g_p = gates[:tok.mp, :2 * nh].reshape(tok.bp, nc, c, 2 * nh).transpose(0, 1, 3, 2)
    g_s = gates[tok.mp:, :2 * nh].reshape(bs, tok.ls, 2 * nh).transpose(0, 2, 1)
    own = (jnp.arange(nb)[None, :] == (jnp.arange(bs) % nb)[:, None])[:, None, :, None]
    i_s = jnp.where(own, g_s[:, :nh, None, :], -jnp.inf).reshape(bs, nh, _SUBLANES)
    f_s = jnp.where(own, g_s[:, nh:, None, :], 0.0).reshape(bs, nh, _SUBLANES)
    go = g_o.reshape(1, dv)

    tok_blk = lambda w: pl.BlockSpec((c, nh * w), lambda b, ci: (b * nc + ci, 0))
    gate_blk = lambda first: pl.BlockSpec((None, None, nh, c), lambda b, ci: (b, ci, first, 0))
    out_p, c_p, n_p, m_p = pl.pallas_call(
        functools.partial(_ml_prompt_body, n_heads=nh, dk=dk, dv=dv),
        grid=(tok.bp, nc),
        in_specs=[tok_blk(dk), tok_blk(dk), tok_blk(dv), tok_blk(dv), gate_blk(0), gate_blk(1),
                  pl.BlockSpec((1, dv), lambda b, ci: (0, 0))],
        out_specs=[tok_blk(dv),
                   pl.BlockSpec((None, None, nh, dk, dv), lambda b, ci: (0, b, 0, 0, 0)),
                   pl.BlockSpec((None, None, nh, dk), lambda b, ci: (0, b, 0, 0)),
                   pl.BlockSpec((None, None, nh, 1), lambda b, ci: (0, b, 0, 0))],
        out_shape=(jax.ShapeDtypeStruct((tok.mp, nh * dv), BF16),
                   jax.ShapeDtypeStruct((1, tok.bp, nh, dk, dv), F32),
                   jax.ShapeDtypeStruct((1, tok.bp, nh, dk), F32),
                   jax.ShapeDtypeStruct((1, tok.bp, nh, 1), F32)),
        compiler_params=_params(2),
        name="ml_prompt",
    )(q, k, v, og, g_p, g_p, go)

    smp_blk = lambda w: pl.BlockSpec((_SUBLANES, nh * w), lambda b: (t0 + b, 0))
    sgate_blk = pl.BlockSpec((nb, nh, _SUBLANES), lambda b: (b, 0, 0))
    c_blk = pl.BlockSpec((None, nb, nh, dk, dv), lambda b: (0, b, 0, 0, 0))
    n_blk = pl.BlockSpec((None, nb, nh, dk), lambda b: (0, b, 0, 0))
    m_blk = pl.BlockSpec((None, nb, nh, 1), lambda b: (0, b, 0, 0))
    out_s, c_s, n_s, m_s = pl.pallas_call(
        functools.partial(_ml_sample_body, n_heads=nh, dk=dk, dv=dv, ls=tok.ls),
        grid=(bs // nb,),
        in_specs=[smp_blk(dk), smp_blk(dk), smp_blk(dv), smp_blk(dv), sgate_blk, sgate_blk,
                  pl.BlockSpec((1, dv), lambda b: (0, 0)), c_blk, n_blk, m_blk],
        out_specs=[pl.BlockSpec((_SUBLANES, nh * dv), lambda b: (b, 0)), c_blk, n_blk, m_blk],
        out_shape=(jax.ShapeDtypeStruct((tok.ms, nh * dv), F32),
                   jax.ShapeDtypeStruct(c0.shape, F32), jax.ShapeDtypeStruct(n0.shape, F32),
                   jax.ShapeDtypeStruct(m0.shape + (1,), F32)),
        compiler_params=_params(1),
        name="ml_sample",
    )(q, k, v, og, i_s, f_s, go, c0, n0, m0[..., None])
    return (out_p, out_s.astype(BF16)), (c_p, n_p, m_p[..., 0]), (c_s, n_s, m_s[..., 0])


def kernel(x_prompt, x_sample, state_gla, state_ret, state_mlstm_C, state_mlstm_n, state_mlstm_m, c_prompt, c_sample, w_ada, b_ada, g_mix, g_mlp, w_up, w_down, g_final, w_gla_q, w_gla_k, w_gla_v, w_gla_a1, w_gla_a2, b_gla_a, w_gla_r, b_gla_r, g_gla_o, w_gla_o, w_ret_q, w_ret_k, w_ret_v, w_ret_g, w_ret_o, w_ml_q, w_ml_k, w_ml_v, w_ml_i, b_ml_i, w_ml_f, b_ml_f, w_ml_og, g_ml_o, w_ml_o):
    bp, lp, d = x_prompt.shape
    bs, ls, _ = x_sample.shape
    depth = w_ada.shape[0]
    ms = bs * ls
    tok = _Tok(bp, lp, bs, ls, min(_MM_TILE, ms))
    ntok = _Tok(bp, lp, bs, ls, min(_NORM_TILE, ms))
    tn = min(_MM_TN, d)

    x = jnp.concatenate([x_prompt.reshape(bp * lp, d), x_sample.reshape(ms, d)], axis=0)

    n_c = bp + bs
    rows = -(-n_c // 16) * 16
    c_all = jnp.pad(jnp.concatenate([c_prompt, c_sample], axis=0), ((0, rows - n_c), (0, 0)))
    mod = _Mod(_ada_mod(c_all, w_ada, b_ada, min(_ADA_TN, d)), bp, bs, ls, d)

    new_gla_p, gla_s = [], None
    ret_p = ret_s = ml_p = ml_s = None
    for i in range(depth):
        kind, j = i % 3, i // 3
        h = _norm_mod(x, g_mix[i], mod, i, 1, 0, ntok)
        if kind == 0:
            q = _mm(h, w_gla_q, j, tok, tn=tn, name="gla_q")
            k = _mm(h, w_gla_k, j, tok, tn=tn, name="gla_k")
            v = _mm(h, w_gla_v, j, tok, tn=tn, name="gla_v")
            r = _mm(h, w_gla_r, j, tok, tn=tn, epi=_epi_bias_silu, extras=_bias_extras(b_gla_r[j], tn), name="gla_r")
            la = _gla_decay(h, w_gla_a1[j], w_gla_a2[j], b_gla_a[j], tok)
            nh = state_gla.shape[2]
            og, s_p, gla_s = _gla_mix(q, k, v, la, r, g_gla_o[j], state_gla, j, gla_s, tok, min(4, nh), min(2, nh))
            new_gla_p.append(s_p)
            w_o = w_gla_o
        elif kind == 1:
            q = _mm(h, w_ret_q, j, tok, tn=tn, name="ret_q")
            k = _mm(h, w_ret_k, j, tok, tn=tn, name="ret_k")
            v = _mm(h, w_ret_v, j, tok, tn=tn, name="ret_v")
            g = _mm(h, w_ret_g, j, tok, tn=tn, epi=_epi_silu, name="ret_g")
            nh = state_ret.shape[2]
            og, ret_p, ret_s = _ret_mix(q, k, v, g, state_ret, tok, min(4, nh), min(8, nh))
            w_o = w_ret_o
        else:
            q = _mm(h, w_ml_q, j, tok, tn=tn, name="ml_q")
            k = _mm(h, w_ml_k, j, tok, tn=tn, name="ml_k")
            v = _mm(h, w_ml_v, j, tok, tn=tn, name="ml_v")
            ogate = _mm(h, w_ml_og, j, tok, tn=tn, epi=_epi_sigmoid, name="ml_og")
            gates = _ml_gates(h, w_ml_i[j], b_ml_i[j], w_ml_f[j], b_ml_f[j], tok)
            og, ml_p, ml_s = _ml_mix(q, k, v, ogate, gates, g_ml_o[j], state_mlstm_C, state_mlstm_n,
                                     state_mlstm_m, tok)
            w_o = w_ml_o
        x = _mm_resid(og, w_o, j, x, mod, i, 2, tok, d, name="mix_o")
        hf = _norm_mod(x, g_mlp[i], mod, i, 4, 3, ntok)
        hid = _mm(hf, w_up, i, tok, tn=tn, epi=_epi_relu2, out_dtype=BF16, name="mlp_up")
        x = _mm_resid(hid, w_down, i, x, mod, i, 5, tok, d, name="mlp_down")

    y_p = _final_norm(x, g_final, ntok.tile, 0, ntok.npt).reshape(bp, lp, d)
    y_s = _final_norm(x, g_final, ntok.tile, ntok.npt, ntok.nst).reshape(bs, ls, d)
    return (y_p, y_s, jnp.stack(new_gla_p), gla_s, ret_p, ret_s,
            ml_p[0], ml_s[0], ml_p[1], ml_s[1], ml_p[2], ml_s[2])
```

```python
import functools

import jax
import jax.numpy as jnp
from jax import lax
from jax.experimental import pallas as pl
from jax.experimental.pallas import tpu as pltpu

F32 = jnp.float32
BF16 = jnp.bfloat16

CHUNK = 64
EPS = 1e-6
GLA_TAU = 16.0
ML_GATE_CAP = 15.0
ROPE_BASE = 10000.0
PAST_LEN = 16384

_VMEM_LIMIT_V7X = 60 * 2**20
_LANES = 128
_SUBLANES = 8
_MM_TILE = 512
_NORM_TILE = 256
_MM_TN = 1024
_RESID_TN = 512
_ADA_TN = 512


def _params(n_grid, limit=_VMEM_LIMIT_V7X):
    return pltpu.CompilerParams(dimension_semantics=("arbitrary",) * n_grid, vmem_limit_bytes=limit)


def _dot(a, b):
    return jnp.dot(a, b, preferred_element_type=F32)


def _dot_nt(a, b):
    return lax.dot_general(a, b, (((1,), (1,)), ((), ())), preferred_element_type=F32)


def _dot_tn(a, b):
    return lax.dot_general(a, b, (((0,), (0,)), ((), ())), preferred_element_type=F32)


def _sigmoid(x):
    return 1.0 / (1.0 + jnp.exp(-x))


def _silu(x):
    return x * _sigmoid(x)


def _log_sigmoid(x):
    return jnp.minimum(x, 0.0) - jnp.log1p(jnp.exp(-jnp.abs(x)))


def _softcap(z):
    return ML_GATE_CAP * jnp.tanh(z / ML_GATE_CAP)


class _Tok:
    def __init__(self, bp, lp, bs, ls, tile):
        self.bp, self.lp, self.bs, self.ls = bp, lp, bs, ls
        self.mp, self.ms = bp * lp, bs * ls
        self.m = self.mp + self.ms
        assert lp % tile == 0 and self.ms % tile == 0, (lp, self.ms, tile)
        self.tile = tile
        self.npt = self.mp // tile
        self.nst = self.ms // tile
        self.tpb = lp // tile

    def prompt_idx(self, i):
        return jnp.minimum(i // self.tpb, self.bp - 1)

    def sample_idx(self, i):
        return jnp.maximum(i - self.npt, 0)


class _Mod:
    def __init__(self, mod, bp, ms, d):
        depth = mod.shape[0]
        self.p = mod[:, ms:ms + bp].reshape(depth, bp, 1, mod.shape[2])
        self.s = mod
        self.d = d

    def specs(self, layer, chunk, tok, width, col):
        per = self.d // width
        return [
            (self.p, (None, None, 1, width), lambda i, j: (layer, tok.prompt_idx(i), 0, chunk * per + col(j))),
            (self.s, (None, tok.tile, width), lambda i, j: (layer, tok.sample_idx(i), chunk * per + col(j))),
        ]


def _ada_body(c_ref, w_ref, b_ref, o_ref, sc_ref):
    @pl.when(jnp.logical_and(pl.program_id(0) == 0, pl.program_id(1) == 0))
    def _():
        sc_ref[...] = _silu(c_ref[...]).astype(BF16)

    o_ref[...] = _dot(sc_ref[...], w_ref[...].astype(BF16)) + b_ref[...]


def _ada_mod(c_all, w_ada, b_ada, tn):
    depth, d, n = w_ada.shape
    rows = c_all.shape[0]
    return pl.pallas_call(
        _ada_body,
        grid=(depth, n // tn),
        in_specs=[
            pl.BlockSpec((rows, d), lambda l, j: (0, 0)),
            pl.BlockSpec((None, d, tn), lambda l, j: (l, 0, j)),
            pl.BlockSpec((None, 1, tn), lambda l, j: (l, 0, j)),
        ],
        out_specs=pl.BlockSpec((None, rows, tn), lambda l, j: (l, 0, j)),
        out_shape=jax.ShapeDtypeStruct((depth, rows, n), F32),
        scratch_shapes=[pltpu.VMEM((rows, d), BF16)],
        compiler_params=_params(2),
        name="ada_mod",
    )(c_all, w_ada, b_ada.reshape(depth, 1, n))


def _rms(x, g):
    return x * lax.rsqrt(jnp.mean(x * x, axis=-1, keepdims=True) + EPS) * g


def _norm_mod_body(x_ref, g_ref, scp_ref, scs_ref, shp_ref, shs_ref, o_ref, *, npt):
    i = pl.program_id(0)
    y = _rms(x_ref[...], g_ref[...])

    @pl.when(i < npt)
    def _():
        o_ref[...] = (y * (1.0 + scp_ref[...]) + shp_ref[...]).astype(o_ref.dtype)

    @pl.when(i >= npt)
    def _():
        o_ref[...] = (y * (1.0 + scs_ref[...]) + shs_ref[...]).astype(o_ref.dtype)


def _norm_mod(x, g, mod, layer, scale_chunk, shift_chunk, tok):
    m, d = x.shape
    t = tok.tile
    ex = mod.specs(layer, scale_chunk, tok, d, lambda j: 0) + mod.specs(layer, shift_chunk, tok, d, lambda j: 0)
    return pl.pallas_call(
        functools.partial(_norm_mod_body, npt=tok.npt),
        grid=(m // t,),
        in_specs=[pl.BlockSpec((t, d), lambda i: (i, 0)), pl.BlockSpec((1, d), lambda i: (0, 0))]
        + [pl.BlockSpec(blk, functools.partial(lambda f, i: f(i, 0), f)) for _, blk, f in ex],
        out_specs=pl.BlockSpec((t, d), lambda i: (i, 0)),
        out_shape=jax.ShapeDtypeStruct((m, d), BF16),
        compiler_params=_params(1),
        name="norm_mod",
    )(x, g.reshape(1, d), *[e[0] for e in ex])


def _final_norm_body(x_ref, g_ref, o_ref):
    o_ref[...] = _rms(x_ref[...], g_ref[...])


def _final_norm(x, g, tile, first_tile, n_tiles):
    d = x.shape[1]
    return pl.pallas_call(
        _final_norm_body,
        grid=(n_tiles,),
        in_specs=[pl.BlockSpec((tile, d), lambda i: (i + first_tile, 0)), pl.BlockSpec((1, d), lambda i: (0, 0))],
        out_specs=pl.BlockSpec((tile, d), lambda i: (i, 0)),
        out_shape=jax.ShapeDtypeStruct((n_tiles * tile, d), F32),
        compiler_params=_params(1),
        name="final_norm",
    )(x, g.reshape(1, d))


def _epi_plain(acc, ex, is_prompt):
    return acc


def _epi_bias_silu(acc, ex, is_prompt):
    return _silu(acc + ex[0][...])


def _epi_silu(acc, ex, is_prompt):
    return _silu(acc)


def _epi_sigmoid(acc, ex, is_prompt):
    return _sigmoid(acc)


def _epi_relu2(acc, ex, is_prompt):
    r = jnp.maximum(acc, 0.0)
    return r * r


def _epi_add(acc, ex, is_prompt):
    return ex[0][...] + acc


def _epi_resid(acc, ex, is_prompt):
    x_ref, gp_ref, gs_ref = ex[:3]
    if len(ex) == 4:
        acc = ex[3][...] + acc
    gate = jnp.where(is_prompt, gp_ref[...], gs_ref[...])
    return x_ref[...] + gate * acc


def _mm_body(*refs, epi, n_act, n_ex, npt, cast_w):
    acts, w_ref = refs[:n_act], refs[n_act]
    ex = refs[n_act + 1:n_act + 1 + n_ex]
    o_ref = refs[n_act + 1 + n_ex]
    i = pl.program_id(1)
    wbf_ref = w_ref
    if cast_w:
        wbf_ref = refs[n_act + 2 + n_ex]

        @pl.when(i == 0)
        def _():
            wbf_ref[...] = w_ref[...].astype(BF16)

    def run(a_ref):
        acc = _dot(a_ref[...], wbf_ref[...])
        o_ref[...] = epi(acc, ex, i < npt).astype(o_ref.dtype)

    if n_act == 1:
        run(acts[0])
    else:
        pl.when(i < npt)(lambda: run(acts[0]))
        pl.when(i >= npt)(lambda: run(acts[1]))


def _mm(a, w, layer, tok, *, tn, tk=None, kb=0, w_kb=None, epi=_epi_plain, extras=(), out_dtype=F32, name="mm"):
    acts = a if isinstance(a, tuple) else (a,)
    kdim = acts[0].shape[1]
    n = w.shape[2]
    tm = tok.tile
    tn = min(tn, n)
    tk = kdim if tk is None else tk
    w_kb = kb if w_kb is None else w_kb
    cast_w = w.dtype != BF16
    assert n % tn == 0 and kdim % tk == 0 and w.shape[1] % tk == 0
    if len(acts) == 1:
        a_specs = [pl.BlockSpec((tm, tk), lambda j, i: (i, kb))]
    else:
        a_specs = [pl.BlockSpec((tm, tk), lambda j, i: (jnp.minimum(i, tok.npt - 1), kb)),
                   pl.BlockSpec((tm, tk), lambda j, i: (tok.sample_idx(i), kb))]
    ex_specs = [pl.BlockSpec(blk, functools.partial(lambda f, j, i: f(i, j), f)) for _, blk, f in extras]
    return pl.pallas_call(
        functools.partial(_mm_body, epi=epi, n_act=len(acts), n_ex=len(extras), npt=tok.npt, cast_w=cast_w),
        grid=(n // tn, tok.m // tm),
        in_specs=a_specs + [pl.BlockSpec((None, tk, tn), lambda j, i: (layer, w_kb, j))] + ex_specs,
        out_specs=pl.BlockSpec((tm, tn), lambda j, i: (i, j)),
        out_shape=jax.ShapeDtypeStruct((tok.m, n), out_dtype),
        scratch_shapes=[pltpu.VMEM((tk, tn), BF16)] if cast_w else [],
        compiler_params=_params(2),
        name=name,
    )(*acts, w, *[e[0] for e in extras])


def _tile_extra(t, tok, tn):
    return (t, (tok.tile, tn), lambda i, j: (i, j))


def _last_chunk_bf16(w, d):
    return w[:, w.shape[1] - d:, :].astype(BF16)


def _mm_resid(a, w, w_last, layer, x, mod, mod_layer, gate_chunk, tok, d, name):
    acts = a if isinstance(a, tuple) else (a,)
    kdim = acts[0].shape[1]
    nk = -(-kdim // d)
    tk = kdim // nk
    assert w_last.shape[1] == tk
    part = None
    for kb in range(nk - 1):
        tn = _MM_TN if len(acts) == 1 else _RESID_TN
        tn = min(tn, w.shape[2])
        ex = [] if part is None else [_tile_extra(part, tok, tn)]
        part = _mm(a, w, layer, tok, tn=tn, tk=tk, kb=kb, epi=_epi_plain if part is None else _epi_add, extras=ex,
                   name=name + "_part")
    tn = min(_MM_TN, w.shape[2])
    ex = [_tile_extra(x, tok, tn)] + mod.specs(mod_layer, gate_chunk, tok, tn, lambda j: j)
    if part is not None:
        ex.append(_tile_extra(part, tok, tn))
    return _mm(a, w_last, layer, tok, tn=tn, tk=tk, kb=nk - 1, w_kb=0, epi=_epi_resid, extras=ex, name=name)


def _bias_extras(b, tn):
    return [(b.reshape(1, -1), (1, tn), lambda i, j: (0, j))]


def _gla_decay_body(a_ref, w1_ref, w2_ref, b_ref, o_ref):
    t = _dot(a_ref[...], w1_ref[...].astype(BF16)).astype(BF16)
    z = _dot(t, w2_ref[...].astype(BF16)) + b_ref[...]
    o_ref[...] = _log_sigmoid(z) / GLA_TAU


def _gla_decay(h, w1, w2, b, tok):
    m, d = h.shape
    rank, n = w2.shape
    pad = _LANES - rank
    w1p = jnp.pad(w1, ((0, 0), (0, pad)))
    w2p = jnp.pad(w2, ((0, pad), (0, 0)))
    tm = tok.tile
    return pl.pallas_call(
        _gla_decay_body,
        grid=(m // tm,),
        in_specs=[pl.BlockSpec((tm, d), lambda i: (i, 0)), pl.BlockSpec((d, _LANES), lambda i: (0, 0)),
                  pl.BlockSpec((_LANES, n), lambda i: (0, 0)), pl.BlockSpec((1, n), lambda i: (0, 0))],
        out_specs=pl.BlockSpec((tm, n), lambda i: (i, 0)),
        out_shape=jax.ShapeDtypeStruct((m, n), F32),
        compiler_params=_params(1),
        name="gla_decay",
    )(h, w1p, w2p, b.reshape(1, n))


def _ml_gates_body(a_ref, w_ref, b_ref, o_ref, *, n_heads):
    z = _softcap(_dot(a_ref[...], w_ref[...].astype(BF16)) + b_ref[...])
    col = lax.broadcasted_iota(jnp.int32, z.shape, 1)
    o_ref[...] = jnp.where(col < n_heads, z, _log_sigmoid(z))


def _ml_gates(h, w_i, b_i, w_f, b_f, tok):
    m, d = h.shape
    nh = w_i.shape[1]
    pad = _LANES - 2 * nh
    w = jnp.pad(jnp.concatenate([w_i, w_f], axis=1), ((0, 0), (0, pad)))
    b = jnp.pad(jnp.concatenate([b_i, b_f]), (0, pad)).reshape(1, _LANES)
    tm = tok.tile
    return pl.pallas_call(
        functools.partial(_ml_gates_body, n_heads=nh),
        grid=(m // tm,),
        in_specs=[pl.BlockSpec((tm, d), lambda i: (i, 0)), pl.BlockSpec((d, _LANES), lambda i: (0, 0)),
                  pl.BlockSpec((1, _LANES), lambda i: (0, 0))],
        out_specs=pl.BlockSpec((tm, _LANES), lambda i: (i, 0)),
        out_shape=jax.ShapeDtypeStruct((m, _LANES), F32),
        compiler_params=_params(1),
        name="ml_gates",
    )(h, w, b)


def _causal(c):
    row = lax.broadcasted_iota(jnp.int32, (c, c), 0)
    col = lax.broadcasted_iota(jnp.int32, (c, c), 1)
    return row, col


def _seq_masks(c, ls):
    row = lax.broadcasted_iota(jnp.int32, (c, 1), 0)
    return [jnp.logical_and(row >= s * ls, row < (s + 1) * ls) for s in range(c // ls)]


def _head_norm(o):
    return o * lax.rsqrt(jnp.mean(o * o, axis=-1, keepdims=True) + EPS)


def _gla_chunk(q, k, v, e_b, e_nb, e_eb, decay_col, s):
    c = q.shape[0]
    row, col = _causal(c)
    q_dec = (q * e_b).astype(BF16)
    k_inv = (k * e_nb).astype(BF16)
    k_end = (k * e_eb).astype(BF16)
    vb = v.astype(BF16)
    att = jnp.where(row >= col, _dot_nt(q_dec, k_inv), 0.0)
    o = _dot(att.astype(BF16), vb) + _dot(q_dec, s.astype(BF16))
    s_new = decay_col * s + _dot_tn(k_end, vb)
    return o, s_new


def _gla_one(q, k, v, g, s):
    c = q.shape[0]
    row, col = _causal(c)
    tril = (row >= col).astype(BF16)
    g_hi = g.astype(BF16)
    g_lo = (g - g_hi.astype(F32)).astype(BF16)
    b = _dot(tril, g_hi) + _dot(tril, g_lo)
    b_end = b[c - 1:c, :]
    b_end_col = jnp.transpose(b[c - _SUBLANES:c, :])[:, _SUBLANES - 1:_SUBLANES]
    return _gla_chunk(q, k, v, jnp.exp(b), jnp.exp(-b), jnp.exp(b_end - b), jnp.exp(b_end_col), s)


def _gla_prompt_body(q_ref, k_ref, v_ref, la_ref, r_ref, go_ref, og_ref, s_ref, *, n_heads, dk, dv):
    @pl.when(pl.program_id(2) == 0)
    def _():
        s_ref[...] = jnp.zeros_like(s_ref)

    for hh in range(n_heads):
        sk = slice(hh * dk, (hh + 1) * dk)
        sv = slice(hh * dv, (hh + 1) * dv)
        o, s_new = _gla_one(q_ref[:, sk] * dk ** -0.5, k_ref[:, sk], v_ref[:, sv], la_ref[:, sk], s_ref[hh])
        s_ref[hh] = s_new
        og_ref[:, sv] = (_head_norm(o) * go_ref[...] * r_ref[:, sv]).astype(og_ref.dtype)


def _gla_sample_body(q_ref, k_ref, v_ref, la_ref, r_ref, go_ref, s0_ref, *rest, n_heads, dk, dv, ls):
    og_ref, s_ref = rest[-2], rest[-1]
    masks = _seq_masks(q_ref.shape[0], ls)
    for hh in range(n_heads):
        sk = slice(hh * dk, (hh + 1) * dk)
        sv = slice(hh * dv, (hh + 1) * dv)
        out = jnp.zeros((q_ref.shape[0], dv), F32)
        for bb, m in enumerate(masks):
            z = lambda t: jnp.where(m, t, 0.0)
            o, s_new = _gla_one(z(q_ref[:, sk]) * dk ** -0.5, z(k_ref[:, sk]), z(v_ref[:, sv]), z(la_ref[:, sk]),
                                s0_ref[bb, hh])
            s_ref[bb, hh] = s_new
            out = jnp.where(m, _head_norm(o) * go_ref[...], out)
        og_ref[:, sv] = out * r_ref[:, sv]


def _ret_one(q, k, v, steps, n_valid, lg, cos, sin, s):
    dk = q.shape[1]
    half = dk // 2

    def rot(t):
        t1, t2 = t[:, :half], t[:, half:]
        return jnp.concatenate([t1 * cos - t2 * sin, t1 * sin + t2 * cos], axis=-1)

    b = steps * lg
    b_end = jnp.full((1, 1), n_valid, F32) * lg
    return _gla_chunk(rot(q), rot(k) * dk ** -0.5, v, jnp.exp(b), jnp.exp(-b), jnp.exp(b_end - b), jnp.exp(b_end), s)


def _ret_prompt_body(lg_ref, q_ref, k_ref, v_ref, g_ref, cos_ref, sin_ref, og_ref, s_ref, *, n_heads, dk, dv):
    @pl.when(pl.program_id(2) == 0)
    def _():
        s_ref[...] = jnp.zeros_like(s_ref)

    c = q_ref.shape[0]
    steps = (lax.broadcasted_iota(jnp.int32, (c, 1), 0) + 1).astype(F32)
    head0 = pl.program_id(1) * n_heads
    for hh in range(n_heads):
        sk = slice(hh * dk, (hh + 1) * dk)
        sv = slice(hh * dv, (hh + 1) * dv)
        o, s_new = _ret_one(q_ref[:, sk], k_ref[:, sk], v_ref[:, sv], steps, c, lg_ref[head0 + hh],
                            cos_ref[...], sin_ref[...], s_ref[hh])
        s_ref[hh] = s_new
        og_ref[:, sv] = (g_ref[:, sv] * _head_norm(o)).astype(og_ref.dtype)


def _ret_sample_body(lg_ref, q_ref, k_ref, v_ref, g_ref, cos_ref, sin_ref, s0_ref, og_ref, s_ref, *,
                     head0, n_heads, dk, dv, ls):
    c = q_ref.shape[0]
    masks = _seq_masks(c, ls)
    row = lax.broadcasted_iota(jnp.int32, (c, 1), 0)
    for hh in range(n_heads):
        sk = slice(hh * dk, (hh + 1) * dk)
        sv = slice(hh * dv, (hh + 1) * dv)
        out = jnp.zeros((c, dv), F32)
        for bb, m in enumerate(masks):
            z = lambda t: jnp.where(m, t, 0.0)
            steps = jnp.clip(row - bb * ls + 1, 0, ls).astype(F32)
            o, s_new = _ret_one(z(q_ref[:, sk]), z(k_ref[:, sk]), z(v_ref[:, sv]), steps, ls, lg_ref[head0 + hh],
                                cos_ref[...], sin_ref[...], s0_ref[bb, hh])
            s_ref[bb, hh] = s_new
            out = jnp.where(m, _head_norm(o), out)
        og_ref[:, sv] = g_ref[:, sv] * out


def _ml_one(q, k, v, i_row, f_row, cm, n, m_prev):
    c = q.shape[0]
    row, col = _causal(c)
    causal = row >= col
    eye = row == col
    i_col = jnp.sum(jnp.where(eye, i_row, 0.0), axis=1, keepdims=True)
    f_col = jnp.sum(jnp.where(eye, f_row, 0.0), axis=1, keepdims=True)
    fcum_col = jnp.sum(jnp.where(causal, f_row, 0.0), axis=1, keepdims=True)
    fcum_row = jnp.sum(jnp.where(row <= col, f_col, 0.0), axis=0, keepdims=True)
    logw = jnp.where(causal, fcum_col - fcum_row + i_row, -jnp.inf)
    log_inter = fcum_col + m_prev
    m_t = jnp.maximum(log_inter, jnp.max(logw, axis=1, keepdims=True))
    w = jnp.exp(logw - m_t)
    a = jnp.exp(log_inter - m_t)
    qb = q.astype(BF16)
    vb = v.astype(BF16)
    s = _dot_nt(qb, k.astype(BF16)) * w
    num = _dot(s.astype(BF16), vb) + a * _dot(qb, cm.astype(BF16))
    den = jnp.sum(s, axis=1, keepdims=True) + a * jnp.sum(q * n, axis=1, keepdims=True)
    h = num / jnp.maximum(jnp.abs(den), jnp.exp(-m_t))
    m_new = m_t[c - 1:c, :]
    f_last = fcum_col[c - 1:c, :]
    a_end = jnp.exp(f_last + m_prev - m_new)
    kw = jnp.exp(f_last - fcum_col + i_col - m_new) * k
    c_new = a_end * cm + _dot_tn(kw.astype(BF16), vb)
    n_new = a_end * n + jnp.sum(kw, axis=0, keepdims=True)
    return h, c_new, n_new, m_new


def _ml_prompt_body(q_ref, k_ref, v_ref, og_ref, i_ref, f_ref, go_ref, out_ref, c_ref, n_ref, m_ref, *,
                    n_heads, dk, dv):
    @pl.when(pl.program_id(1) == 0)
    def _():
        c_ref[...] = jnp.zeros_like(c_ref)
        n_ref[...] = jnp.zeros_like(n_ref)
        m_ref[...] = jnp.zeros_like(m_ref)

    for hh in range(n_heads):
        sk = slice(hh * dk, (hh + 1) * dk)
        sv = slice(hh * dv, (hh + 1) * dv)
        h, c_new, n_new, m_new = _ml_one(q_ref[:, sk], k_ref[:, sk] * dk ** -0.5, v_ref[:, sv],
                                         i_ref[hh:hh + 1, :], f_ref[hh:hh + 1, :],
                                         c_ref[hh], n_ref[hh:hh + 1, :], m_ref[hh:hh + 1, :])
        c_ref[hh] = c_new
        n_ref[hh:hh + 1, :] = n_new
        m_ref[hh:hh + 1, :] = m_new
        out_ref[:, sv] = (og_ref[:, sv] * (_head_norm(h) * go_ref[...])).astype(out_ref.dtype)


def _ml_sample_body(q_ref, k_ref, v_ref, og_ref, i_ref, f_ref, go_ref, c0_ref, n0_ref, m0_ref,
                    out_ref, c_ref, n_ref, m_ref, *, n_heads, dk, dv, ls):
    rows = q_ref.shape[0]
    masks = _seq_masks(rows, ls)
    for hh in range(n_heads):
        sk = slice(hh * dk, (hh + 1) * dk)
        sv = slice(hh * dv, (hh + 1) * dv)
        out = jnp.zeros((rows, dv), F32)
        for bb, m in enumerate(masks):
            z = lambda t: jnp.where(m, t, 0.0)
            h, c_new, n_new, m_new = _ml_one(z(q_ref[:, sk]), z(k_ref[:, sk]) * dk ** -0.5, z(v_ref[:, sv]),
                                             i_ref[bb, hh:hh + 1, :], f_ref[bb, hh:hh + 1, :],
                                             c0_ref[bb, hh], n0_ref[bb, hh:hh + 1, :], m0_ref[bb, hh:hh + 1, :])
            c_ref[bb, hh] = c_new
            n_ref[bb, hh:hh + 1, :] = n_new
            m_ref[bb, hh:hh + 1, :] = m_new
            out = jnp.where(m, _head_norm(h) * go_ref[...], out)
        out_ref[:, sv] = og_ref[:, sv] * out


def _chunking(tok):
    c = CHUNK if tok.lp % CHUNK == 0 else tok.lp
    return c, tok.lp // c


def _pair_rows(tok):
    assert _SUBLANES % tok.ls == 0 and tok.mp % _SUBLANES == 0
    nb = _SUBLANES // tok.ls
    assert tok.bs % nb == 0
    return nb, tok.mp // _SUBLANES


def _sample_schedule(n_prompt_steps, pairs, nh, min_group):
    for gs in range(min_group, nh + 1):
        n_sample_steps = pairs * (nh // gs)
        if nh % gs == 0 and n_prompt_steps % n_sample_steps == 0:
            return gs, n_prompt_steps // n_sample_steps
    raise ValueError((n_prompt_steps, pairs, nh))


def _gla_fused_body(qp, kp, vp, lap, rp, go, qs, ks, vs, las, rs, s0, *rest, gp, gs, dk, dv, ls, ratio, nhg_p, nc):
    og_p, s_p, og_s, s_s = rest[-4:]
    _gla_prompt_body(qp, kp, vp, lap, rp, go, og_p, s_p, n_heads=gp, dk=dk, dv=dv)
    t = (pl.program_id(0) * nhg_p + pl.program_id(1)) * nc + pl.program_id(2)

    @pl.when(t % ratio == 0)
    def _():
        _gla_sample_body(qs, ks, vs, las, rs, go, s0, og_s, s_s, n_heads=gs, dk=dk, dv=dv, ls=ls)


def _gla_mix(q, k, v, la, r, g_o, state, layer, prev_states, tok, group_p):
    nl, bs, nh, dk, dv = state.shape
    c, nc = _chunking(tok)
    gp = group_p
    nhg_p = nh // gp
    nb, t0 = _pair_rows(tok)
    gs, ratio = _sample_schedule(tok.bp * nhg_p * nc, bs // nb, nh, 1)
    nhg_s = nh // gs

    def smp(b, hg, ci):
        u = ((b * nhg_p + hg) * nc + ci) // ratio
        return u // nhg_s, u % nhg_s

    def at_sample(f):
        return lambda b, hg, ci: f(*smp(b, hg, ci))

    tok_blk = lambda w: pl.BlockSpec((c, gp * w), lambda b, hg, ci: (b * nc + ci, hg))
    smp_blk = lambda w: pl.BlockSpec((_SUBLANES, gs * w), at_sample(lambda p, h: (t0 + p, h)))
    st_blk = pl.BlockSpec((None, nb, gs, dk, dv), at_sample(lambda p, h: (layer, p, h, 0, 0)))
    go = g_o.reshape(1, dv)
    args = [q, k, v, la, r, go, q, k, v, la, r, state]
    in_specs = [tok_blk(dk), tok_blk(dk), tok_blk(dv), tok_blk(dk), tok_blk(dv),
                pl.BlockSpec((1, dv), lambda b, hg, ci: (0, 0)),
                smp_blk(dk), smp_blk(dk), smp_blk(dv), smp_blk(dk), smp_blk(dv), st_blk]
    aliases = {}
    if prev_states is not None:
        args += list(prev_states)
        in_specs += [pl.BlockSpec(memory_space=pl.ANY)] * 2
        aliases = {len(args) - 2: 1, len(args) - 1: 3}
    og_p, s_p, og_s, s_s = pl.pallas_call(
        functools.partial(_gla_fused_body, gp=gp, gs=gs, dk=dk, dv=dv, ls=tok.ls, ratio=ratio, nhg_p=nhg_p, nc=nc),
        grid=(tok.bp, nhg_p, nc),
        in_specs=in_specs,
        out_specs=[tok_blk(dv), pl.BlockSpec((None, None, gp, dk, dv), lambda b, hg, ci: (layer, b, hg, 0, 0)),
                   pl.BlockSpec((_SUBLANES, gs * dv), at_sample(lambda p, h: (p, h))), st_blk],
        out_shape=(jax.ShapeDtypeStruct((tok.mp, nh * dv), BF16),
                   jax.ShapeDtypeStruct((nl, tok.bp, nh, dk, dv), F32),
                   jax.ShapeDtypeStruct((tok.ms, nh * dv), F32), jax.ShapeDtypeStruct(state.shape, F32)),
        input_output_aliases=aliases,
        compiler_params=_params(3),
        name="gla_mix",
    )(*args)
    return (og_p, og_s.astype(BF16)), (s_p, s_s)


def _rope_tables(pos, half):
    inv = ROPE_BASE ** (-jnp.arange(half, dtype=F32) / half)
    ang = pos[:, None] * inv[None, :]
    return jnp.cos(ang), jnp.sin(ang)


def _ret_fused_body(lg, qp, kp, vp, gp_ref, cosp, sinp, qs, ks, vs, gs_ref, coss, sins, s0, og_p, s_p, og_s, s_s, *,
                    gp, gs, dk, dv, ls, ratio, nhg_p, nhg_s, nc):
    _ret_prompt_body(lg, qp, kp, vp, gp_ref, cosp, sinp, og_p, s_p, n_heads=gp, dk=dk, dv=dv)
    t = (pl.program_id(0) * nhg_p + pl.program_id(1)) * nc + pl.program_id(2)

    @pl.when(t % ratio == 0)
    def _():
        _ret_sample_body(lg, qs, ks, vs, gs_ref, coss, sins, s0, og_s, s_s,
                         head0=((t // ratio) % nhg_s) * gs, n_heads=gs, dk=dk, dv=dv, ls=ls)


def _ret_mix(q, k, v, g, state, tok, group_p):
    _, bs, nh, dk, dv = state.shape
    c, nc = _chunking(tok)
    half = dk // 2
    log_gamma = jnp.log1p(-jnp.exp2(-5.0 - jnp.arange(nh, dtype=F32)))
    cos_p, sin_p = _rope_tables(0 + jnp.arange(tok.lp, dtype=F32), half)
    cos_s, sin_s = _rope_tables(PAST_LEN + jnp.arange(tok.ls, dtype=F32), half)
    gp = group_p
    nhg_p = nh // gp
    nb, t0 = _pair_rows(tok)
    gs, ratio = _sample_schedule(tok.bp * nhg_p * nc, bs // nb, nh, 1)
    nhg_s = nh // gs

    def at_sample(f):
        def index(b, hg, ci, lg):
            u = ((b * nhg_p + hg) * nc + ci) // ratio
            return f(u // nhg_s, u % nhg_s)
        return index

    tok_blk = lambda w: pl.BlockSpec((c, gp * w), lambda b, hg, ci, lg: (b * nc + ci, hg))
    rope_blk = pl.BlockSpec((c, half), lambda b, hg, ci, lg: (ci, 0))
    smp_blk = lambda w: pl.BlockSpec((_SUBLANES, gs * w), at_sample(lambda p, h: (t0 + p, h)))
    st_blk = pl.BlockSpec((None, nb, gs, dk, dv), at_sample(lambda p, h: (0, p, h, 0, 0)))
    rope_s = pl.BlockSpec((_SUBLANES, half), lambda b, hg, ci, lg: (0, 0))
    og_p, s_p, og_s, s_s = pl.pallas_call(
        functools.partial(_ret_fused_body, gp=gp, gs=gs, dk=dk, dv=dv, ls=tok.ls, ratio=ratio, nhg_p=nhg_p,
                          nhg_s=nhg_s, nc=nc),
        grid_spec=pltpu.PrefetchScalarGridSpec(
            num_scalar_prefetch=1,
            grid=(tok.bp, nhg_p, nc),
            in_specs=[tok_blk(dk), tok_blk(dk), tok_blk(dv), tok_blk(dv), rope_blk, rope_blk,
                      smp_blk(dk), smp_blk(dk), smp_blk(dv), smp_blk(dv), rope_s, rope_s, st_blk],
            out_specs=[tok_blk(dv), pl.BlockSpec((None, None, gp, dk, dv), lambda b, hg, ci, lg: (0, b, hg, 0, 0)),
                       pl.BlockSpec((_SUBLANES, gs * dv), at_sample(lambda p, h: (p, h))), st_blk],
        ),
        out_shape=(jax.ShapeDtypeStruct((tok.mp, nh * dv), BF16), jax.ShapeDtypeStruct((1, tok.bp, nh, dk, dv), F32),
                   jax.ShapeDtypeStruct((tok.ms, nh * dv), F32), jax.ShapeDtypeStruct(state.shape, F32)),
        compiler_params=_params(3),
        name="ret_mix",
    )(log_gamma, q, k, v, g, cos_p, sin_p, q, k, v, g, jnp.tile(cos_s, (nb, 1)), jnp.tile(sin_s, (nb, 1)), state)
    return (og_p, og_s.astype(BF16)), s_p, s_s


def _ml_fused_body(qp, kp, vp, ogp, ip, fp, go, qs, ks, vs, ogs, i_s, f_s, c0, n0, m0,
                   out_p, c_p, n_p, m_p, out_s, c_s, n_s, m_s, *, n_heads, dk, dv, ls, ratio, nc):
    _ml_prompt_body(qp, kp, vp, ogp, ip, fp, go, out_p, c_p, n_p, m_p, n_heads=n_heads, dk=dk, dv=dv)
    t = pl.program_id(0) * nc + pl.program_id(1)

    @pl.when(t % ratio == 0)
    def _():
        _ml_sample_body(qs, ks, vs, ogs, i_s, f_s, go, c0, n0, m0, out_s, c_s, n_s, m_s,
                        n_heads=n_heads, dk=dk, dv=dv, ls=ls)


def _ml_mix(q, k, v, og, gates, g_o, c0, n0, m0, tok):
    _, bs, nh, dk, dv = c0.shape
    c, nc = _chunking(tok)
    nb, t0 = _pair_rows(tok)
    g_p = gates[:tok.mp, :2 * nh].reshape(tok.bp, nc, c, 2 * nh).transpose(0, 1, 3, 2)
    g_s = gates[tok.mp:, :2 * nh].reshape(bs, tok.ls, 2 * nh).transpose(0, 2, 1)
    own = (jnp.arange(nb)[None, :] == (jnp.arange(bs) % nb)[:, None])[:, None, :, None]
    i_s = jnp.where(own, g_s[:, :nh, None, :], -jnp.inf).reshape(bs, nh, _SUBLANES)
    f_s = jnp.where(own, g_s[:, nh:, None, :], 0.0).reshape(bs, nh, _SUBLANES)
    go = g_o.reshape(1, dv)

    _, ratio = _sample_schedule(tok.bp * nc, bs // nb, nh, nh)
    pair = lambda b, ci: (b * nc + ci) // ratio
    tok_blk = lambda w: pl.BlockSpec((c, nh * w), lambda b, ci: (b * nc + ci, 0))
    gate_blk = lambda first: pl.BlockSpec((None, None, nh, c), lambda b, ci: (b, ci, first, 0))
    smp_blk = lambda w: pl.BlockSpec((_SUBLANES, nh * w), lambda b, ci: (t0 + pair(b, ci), 0))
    sgate_blk = pl.BlockSpec((nb, nh, _SUBLANES), lambda b, ci: (pair(b, ci), 0, 0))
    c_blk = pl.BlockSpec((None, nb, nh, dk, dv), lambda b, ci: (0, pair(b, ci), 0, 0, 0))
    n_blk = pl.BlockSpec((None, nb, nh, dk), lambda b, ci: (0, pair(b, ci), 0, 0))
    m_blk = pl.BlockSpec((None, nb, nh, 1), lambda b, ci: (0, pair(b, ci), 0, 0))
    out_p, c_p, n_p, m_p, out_s, c_s, n_s, m_s = pl.pallas_call(
        functools.partial(_ml_fused_body, n_heads=nh, dk=dk, dv=dv, ls=tok.ls, ratio=ratio, nc=nc),
        grid=(tok.bp, nc),
        in_specs=[tok_blk(dk), tok_blk(dk), tok_blk(dv), tok_blk(dv), gate_blk(0), gate_blk(1),
                  pl.BlockSpec((1, dv), lambda b, ci: (0, 0)),
                  smp_blk(dk), smp_blk(dk), smp_blk(dv), smp_blk(dv), sgate_blk, sgate_blk, c_blk, n_blk, m_blk],
        out_specs=[tok_blk(dv),
                   pl.BlockSpec((None, None, nh, dk, dv), lambda b, ci: (0, b, 0, 0, 0)),
                   pl.BlockSpec((None, None, nh, dk), lambda b, ci: (0, b, 0, 0)),
                   pl.BlockSpec((None, None, nh, 1), lambda b, ci: (0, b, 0, 0)),
                   pl.BlockSpec((_SUBLANES, nh * dv), lambda b, ci: (pair(b, ci), 0)), c_blk, n_blk, m_blk],
        out_shape=(jax.ShapeDtypeStruct((tok.mp, nh * dv), BF16),
                   jax.ShapeDtypeStruct((1, tok.bp, nh, dk, dv), F32),
                   jax.ShapeDtypeStruct((1, tok.bp, nh, dk), F32),
                   jax.ShapeDtypeStruct((1, tok.bp, nh, 1), F32),
                   jax.ShapeDtypeStruct((tok.ms, nh * dv), F32),
                   jax.ShapeDtypeStruct(c0.shape, F32), jax.ShapeDtypeStruct(n0.shape, F32),
                   jax.ShapeDtypeStruct(m0.shape + (1,), F32)),
        compiler_params=_params(2),
        name="ml_mix",
    )(q, k, v, og, g_p, g_p, go, q, k, v, og, i_s, f_s, c0, n0, m0[..., None])
    return (out_p, out_s.astype(BF16)), (c_p, n_p, m_p[..., 0]), (c_s, n_s, m_s[..., 0])


def kernel(x_prompt, x_sample, state_gla, state_ret, state_mlstm_C, state_mlstm_n, state_mlstm_m, c_prompt, c_sample, w_ada, b_ada, g_mix, g_mlp, w_up, w_down, g_final, w_gla_q, w_gla_k, w_gla_v, w_gla_a1, w_gla_a2, b_gla_a, w_gla_r, b_gla_r, g_gla_o, w_gla_o, w_ret_q, w_ret_k, w_ret_v, w_ret_g, w_ret_o, w_ml_q, w_ml_k, w_ml_v, w_ml_i, b_ml_i, w_ml_f, b_ml_f, w_ml_og, g_ml_o, w_ml_o):
    bp, lp, d = x_prompt.shape
    bs, ls, _ = x_sample.shape
    depth = w_ada.shape[0]
    ms = bs * ls
    tok = _Tok(bp, lp, bs, ls, min(_MM_TILE, ms))
    ntok = _Tok(bp, lp, bs, ls, min(_NORM_TILE, ms))
    tn = min(_MM_TN, d)

    x = jnp.concatenate([x_prompt.reshape(bp * lp, d), x_sample.reshape(ms, d)], axis=0)

    n_c = ms + bp
    rows = -(-n_c // 16) * 16
    c_all = jnp.pad(jnp.concatenate([jnp.repeat(c_sample, ls, axis=0), c_prompt], axis=0), ((0, rows - n_c), (0, 0)))
    mod = _Mod(_ada_mod(c_all, w_ada, b_ada, min(_ADA_TN, d)), bp, ms, d)

    w_down_last = _last_chunk_bf16(w_down, d)
    w_o_last = [_last_chunk_bf16(w, d) for w in (w_gla_o, w_ret_o, w_ml_o)]

    gla_states = None
    ret_p = ret_s = ml_p = ml_s = None
    for i in range(depth):
        kind, j = i % 3, i // 3
        h = _norm_mod(x, g_mix[i], mod, i, 1, 0, ntok)
        if kind == 0:
            q = _mm(h, w_gla_q, j, tok, tn=tn, name="gla_q")
            k = _mm(h, w_gla_k, j, tok, tn=tn, name="gla_k")
            v = _mm(h, w_gla_v, j, tok, tn=tn, name="gla_v")
            r = _mm(h, w_gla_r, j, tok, tn=tn, epi=_epi_bias_silu, extras=_bias_extras(b_gla_r[j], tn), name="gla_r")
            la = _gla_decay(h, w_gla_a1[j], w_gla_a2[j], b_gla_a[j], tok)
            nh = state_gla.shape[2]
            og, gla_states = _gla_mix(q, k, v, la, r, g_gla_o[j], state_gla, j, gla_states, tok, min(2, nh))
            w_o = w_gla_o
        elif kind == 1:
            q = _mm(h, w_ret_q, j, tok, tn=tn, name="ret_q")
            k = _mm(h, w_ret_k, j, tok, tn=tn, name="ret_k")
            v = _mm(h, w_ret_v, j, tok, tn=tn, name="ret_v")
            g = _mm(h, w_ret_g, j, tok, tn=tn, epi=_epi_silu, name="ret_g")
            nh = state_ret.shape[2]
            og, ret_p, ret_s = _ret_mix(q, k, v, g, state_ret, tok, min(8, nh))
            w_o = w_ret_o
        else:
            q = _mm(h, w_ml_q, j, tok, tn=tn, name="ml_q")
            k = _mm(h, w_ml_k, j, tok, tn=tn, name="ml_k")
            v = _mm(h, w_ml_v, j, tok, tn=tn, name="ml_v")
            ogate = _mm(h, w_ml_og, j, tok, tn=tn, epi=_epi_sigmoid, name="ml_og")
            gates = _ml_gates(h, w_ml_i[j], b_ml_i[j], w_ml_f[j], b_ml_f[j], tok)
            og, ml_p, ml_s = _ml_mix(q, k, v, ogate, gates, g_ml_o[j], state_mlstm_C, state_mlstm_n,
                                     state_mlstm_m, tok)
            w_o = w_ml_o
        x = _mm_resid(og, w_o, w_o_last[kind], j, x, mod, i, 2, tok, d, name="mix_o")
        hf = _norm_mod(x, g_mlp[i], mod, i, 4, 3, ntok)
        hid = _mm(hf, w_up, i, tok, tn=tn, epi=_epi_relu2, out_dtype=BF16, name="mlp_up")
        x = _mm_resid(hid, w_down, w_down_last, i, x, mod, i, 5, tok, d, name="mlp_down")

    y_p = _final_norm(x, g_final, ntok.tile, 0, ntok.npt).reshape(bp, lp, d)
    y_s = _final_norm(x, g_final, ntok.tile, ntok.npt, ntok.nst).reshape(bs, ls, d)
    return (y_p, y_s, gla_states[0], gla_states[1], ret_p, ret_s,
            ml_p[0], ml_s[0], ml_p[1], ml_s[1], ml_p[2], ml_s[2])
```

```python
import functools

import jax
import jax.numpy as jnp
from jax import lax
from jax.experimental import pallas as pl
from jax.experimental.pallas import tpu as pltpu

F32 = jnp.float32
BF16 = jnp.bfloat16

CHUNK = 64
EPS = 1e-6
GLA_TAU = 16.0
ML_GATE_CAP = 15.0
ROPE_BASE = 10000.0
PAST_LEN = 16384

_VMEM_LIMIT_V7X = 60 * 2**20
_LANES = 128
_SUBLANES = 8
_MM_TILE = 512
_NORM_TILE = 256
_MM_TN = 1024
_RESID_TN = 512
_ADA_TN = 512


def _params(n_grid, limit=_VMEM_LIMIT_V7X):
    return pltpu.CompilerParams(dimension_semantics=("arbitrary",) * n_grid, vmem_limit_bytes=limit)


def _dot(a, b):
    return jnp.dot(a, b, preferred_element_type=F32)


def _dot_nt(a, b):
    return lax.dot_general(a, b, (((1,), (1,)), ((), ())), preferred_element_type=F32)


def _dot_tn(a, b):
    return lax.dot_general(a, b, (((0,), (0,)), ((), ())), preferred_element_type=F32)


def _sigmoid(x):
    return 1.0 / (1.0 + jnp.exp(-x))


def _silu(x):
    return x * _sigmoid(x)


def _log_sigmoid(x):
    return jnp.minimum(x, 0.0) - jnp.log1p(jnp.exp(-jnp.abs(x)))


def _softcap(z):
    return ML_GATE_CAP * jnp.tanh(z / ML_GATE_CAP)


class _Tok:
    def __init__(self, bp, lp, bs, ls, tile):
        self.bp, self.lp, self.bs, self.ls = bp, lp, bs, ls
        self.mp, self.ms = bp * lp, bs * ls
        self.m = self.mp + self.ms
        assert lp % tile == 0 and self.ms % tile == 0, (lp, self.ms, tile)
        self.tile = tile
        self.npt = self.mp // tile
        self.nst = self.ms // tile
        self.tpb = lp // tile

    def prompt_idx(self, i):
        return jnp.minimum(i // self.tpb, self.bp - 1)

    def sample_idx(self, i):
        return jnp.maximum(i - self.npt, 0)


class _Mod:
    def __init__(self, mod, bp, ms, d):
        depth = mod.shape[0]
        self.p = mod[:, ms:ms + bp].reshape(depth, bp, 1, mod.shape[2])
        self.s = mod
        self.d = d

    def specs(self, layer, chunk, tok, width, col):
        per = self.d // width
        return [
            (self.p, (None, None, 1, width), lambda i, j: (layer, tok.prompt_idx(i), 0, chunk * per + col(j))),
            (self.s, (None, tok.tile, width), lambda i, j: (layer, tok.sample_idx(i), chunk * per + col(j))),
        ]


def _ada_body(c_ref, w_ref, b_ref, o_ref, sc_ref, *, bs, ls):
    @pl.when(jnp.logical_and(pl.program_id(0) == 0, pl.program_id(1) == 0))
    def _():
        sc_ref[...] = _silu(c_ref[...]).astype(BF16)

    res = _dot(sc_ref[...], w_ref[...].astype(BF16)) + b_ref[...]
    ms = bs * ls
    rs = res[:bs]
    hi = rs.astype(BF16)
    r1 = rs - hi.astype(F32)
    mid = r1.astype(BF16)
    lo = (r1 - mid.astype(F32)).astype(BF16)
    row = lax.broadcasted_iota(jnp.int32, (ms, bs), 0)
    col = lax.broadcasted_iota(jnp.int32, (ms, bs), 1)
    sel = jnp.logical_and(row >= col * ls, row < (col + 1) * ls).astype(BF16)
    o_ref[:ms, :] = (_dot(sel, hi) + _dot(sel, mid)) + _dot(sel, lo)
    o_ref[ms:, :] = res[bs:]


def _ada_mod(c_all, w_ada, b_ada, bs, ls, tn):
    depth, d, n = w_ada.shape
    rows = c_all.shape[0]
    rows_out = bs * ls + rows - bs
    return pl.pallas_call(
        functools.partial(_ada_body, bs=bs, ls=ls),
        grid=(depth, n // tn),
        in_specs=[
            pl.BlockSpec((rows, d), lambda l, j: (0, 0)),
            pl.BlockSpec((None, d, tn), lambda l, j: (l, 0, j)),
            pl.BlockSpec((None, 1, tn), lambda l, j: (l, 0, j)),
        ],
        out_specs=pl.BlockSpec((None, rows_out, tn), lambda l, j: (l, 0, j)),
        out_shape=jax.ShapeDtypeStruct((depth, rows_out, n), F32),
        scratch_shapes=[pltpu.VMEM((rows, d), BF16)],
        compiler_params=_params(2),
        name="ada_mod",
    )(c_all, w_ada, b_ada.reshape(depth, 1, n))


def _rms(x, g):
    return x * lax.rsqrt(jnp.mean(x * x, axis=-1, keepdims=True) + EPS) * g


def _norm_mod_body(*refs, npt, n_x):
    xs = refs[:n_x]
    g_ref, scp_ref, scs_ref, shp_ref, shs_ref, o_ref = refs[n_x:]
    i = pl.program_id(0)

    @pl.when(i < npt)
    def _():
        y = _rms(xs[0][...], g_ref[...])
        o_ref[...] = (y * (1.0 + scp_ref[...]) + shp_ref[...]).astype(o_ref.dtype)

    @pl.when(i >= npt)
    def _():
        y = _rms(xs[-1][...], g_ref[...])
        o_ref[...] = (y * (1.0 + scs_ref[...]) + shs_ref[...]).astype(o_ref.dtype)


def _row_specs(x, tok, width, col):
    if isinstance(x, tuple):
        return [(x[0], (tok.tile, width), lambda i, j: (jnp.minimum(i, tok.npt - 1), col(j))),
                (x[1], (tok.tile, width), lambda i, j: (tok.sample_idx(i), col(j)))]
    return [(x, (tok.tile, width), lambda i, j: (i, col(j)))]


def _norm_mod(x, g, mod, layer, scale_chunk, shift_chunk, tok):
    d = g.shape[0]
    t = tok.tile
    xr = _row_specs(x, tok, d, lambda j: 0)
    ex = xr + [(g.reshape(1, d), (1, d), lambda i, j: (0, 0))]
    ex += mod.specs(layer, scale_chunk, tok, d, lambda j: 0) + mod.specs(layer, shift_chunk, tok, d, lambda j: 0)
    return pl.pallas_call(
        functools.partial(_norm_mod_body, npt=tok.npt, n_x=len(xr)),
        grid=(tok.m // t,),
        in_specs=[pl.BlockSpec(blk, functools.partial(lambda f, i: f(i, 0), f)) for _, blk, f in ex],
        out_specs=pl.BlockSpec((t, d), lambda i: (i, 0)),
        out_shape=jax.ShapeDtypeStruct((tok.m, d), BF16),
        compiler_params=_params(1),
        name="norm_mod",
    )(*[e[0] for e in ex])


def _final_norm_body(x_ref, g_ref, o_ref):
    o_ref[...] = _rms(x_ref[...], g_ref[...])


def _final_norm(x, g, tile, first_tile, n_tiles):
    d = x.shape[1]
    return pl.pallas_call(
        _final_norm_body,
        grid=(n_tiles,),
        in_specs=[pl.BlockSpec((tile, d), lambda i: (i + first_tile, 0)), pl.BlockSpec((1, d), lambda i: (0, 0))],
        out_specs=pl.BlockSpec((tile, d), lambda i: (i, 0)),
        out_shape=jax.ShapeDtypeStruct((n_tiles * tile, d), F32),
        compiler_params=_params(1),
        name="final_norm",
    )(x, g.reshape(1, d))


def _epi_plain(acc, ex, is_prompt):
    return acc


def _epi_bias_silu(acc, ex, is_prompt):
    return _silu(acc + ex[0][...])


def _epi_silu(acc, ex, is_prompt):
    return _silu(acc)


def _epi_sigmoid(acc, ex, is_prompt):
    return _sigmoid(acc)


def _epi_relu2(acc, ex, is_prompt):
    r = jnp.maximum(acc, 0.0)
    return r * r


def _epi_add(acc, ex, is_prompt):
    return ex[0][...] + acc


def _epi_resid(n_x, has_part):
    def epi(acc, ex, is_prompt):
        gp_ref, gs_ref = ex[n_x:n_x + 2]
        if has_part:
            acc = ex[n_x + 2][...] + acc
        x = ex[0][...] if n_x == 1 else jnp.where(is_prompt, ex[0][...], ex[1][...])
        return x + jnp.where(is_prompt, gp_ref[...], gs_ref[...]) * acc

    return epi


def _mm_body(*refs, epi, n_act, n_ex, npt, cast_w):
    acts, w_ref = refs[:n_act], refs[n_act]
    ex = refs[n_act + 1:n_act + 1 + n_ex]
    o_ref = refs[n_act + 1 + n_ex]
    i = pl.program_id(1)
    wbf_ref = w_ref
    if cast_w:
        wbf_ref = refs[n_act + 2 + n_ex]

        @pl.when(i == 0)
        def _():
            wbf_ref[...] = w_ref[...].astype(BF16)

    def run(a_ref):
        acc = _dot(a_ref[...], wbf_ref[...])
        o_ref[...] = epi(acc, ex, i < npt).astype(o_ref.dtype)

    if n_act == 1:
        run(acts[0])
    else:
        pl.when(i < npt)(lambda: run(acts[0]))
        pl.when(i >= npt)(lambda: run(acts[1]))


def _mm(a, w, layer, tok, *, tn, tk=None, kb=0, w_kb=None, epi=_epi_plain, extras=(), out_dtype=F32, name="mm"):
    acts = a if isinstance(a, tuple) else (a,)
    kdim = acts[0].shape[1]
    n = w.shape[2]
    tm = tok.tile
    tn = min(tn, n)
    tk = kdim if tk is None else tk
    w_kb = kb if w_kb is None else w_kb
    cast_w = w.dtype != BF16
    assert n % tn == 0 and kdim % tk == 0 and w.shape[1] % tk == 0
    if len(acts) == 1:
        a_specs = [pl.BlockSpec((tm, tk), lambda j, i: (i, kb))]
    else:
        a_specs = [pl.BlockSpec((tm, tk), lambda j, i: (jnp.minimum(i, tok.npt - 1), kb)),
                   pl.BlockSpec((tm, tk), lambda j, i: (tok.sample_idx(i), kb))]
    ex_specs = [pl.BlockSpec(blk, functools.partial(lambda f, j, i: f(i, j), f)) for _, blk, f in extras]
    return pl.pallas_call(
        functools.partial(_mm_body, epi=epi, n_act=len(acts), n_ex=len(extras), npt=tok.npt, cast_w=cast_w),
        grid=(n // tn, tok.m // tm),
        in_specs=a_specs + [pl.BlockSpec((None, tk, tn), lambda j, i: (layer, w_kb, j))] + ex_specs,
        out_specs=pl.BlockSpec((tm, tn), lambda j, i: (i, j)),
        out_shape=jax.ShapeDtypeStruct((tok.m, n), out_dtype),
        scratch_shapes=[pltpu.VMEM((tk, tn), BF16)] if cast_w else [],
        compiler_params=_params(2),
        name=name,
    )(*acts, w, *[e[0] for e in extras])


def _tile_extra(t, tok, tn):
    return (t, (tok.tile, tn), lambda i, j: (i, j))


def _last_chunk_bf16(w, d):
    return w[:, w.shape[1] - d:, :].astype(BF16)


def _mm_resid(a, w, w_last, layer, x, mod, mod_layer, gate_chunk, tok, d, name):
    acts = a if isinstance(a, tuple) else (a,)
    kdim = acts[0].shape[1]
    nk = -(-kdim // d)
    tk = kdim // nk
    assert w_last.shape[1] == tk
    part = None
    for kb in range(nk - 1):
        tn = _MM_TN if len(acts) == 1 else _RESID_TN
        tn = min(tn, w.shape[2])
        ex = [] if part is None else [_tile_extra(part, tok, tn)]
        part = _mm(a, w, layer, tok, tn=tn, tk=tk, kb=kb, epi=_epi_plain if part is None else _epi_add, extras=ex,
                   name=name + "_part")
    tn = min(_MM_TN, w.shape[2])
    xr = _row_specs(x, tok, tn, lambda j: j)
    ex = xr + mod.specs(mod_layer, gate_chunk, tok, tn, lambda j: j)
    if part is not None:
        ex.append(_tile_extra(part, tok, tn))
    return _mm(a, w_last, layer, tok, tn=tn, tk=tk, kb=nk - 1, w_kb=0, epi=_epi_resid(len(xr), part is not None),
               extras=ex, name=name)


def _bias_extras(b, tn):
    return [(b.reshape(1, -1), (1, tn), lambda i, j: (0, j))]


def _gla_decay_body(a_ref, w1_ref, w2_ref, b_ref, o_ref):
    t = _dot(a_ref[...], w1_ref[...].astype(BF16)).astype(BF16)
    z = _dot(t, w2_ref[...].astype(BF16)) + b_ref[...]
    o_ref[...] = _log_sigmoid(z) / GLA_TAU


def _gla_decay(h, w1, w2, b, tok):
    m, d = h.shape
    rank, n = w2.shape
    pad = _LANES - rank
    w1p = jnp.pad(w1, ((0, 0), (0, pad)))
    w2p = jnp.pad(w2, ((0, pad), (0, 0)))
    tm = tok.tile
    return pl.pallas_call(
        _gla_decay_body,
        grid=(m // tm,),
        in_specs=[pl.BlockSpec((tm, d), lambda i: (i, 0)), pl.BlockSpec((d, _LANES), lambda i: (0, 0)),
                  pl.BlockSpec((_LANES, n), lambda i: (0, 0)), pl.BlockSpec((1, n), lambda i: (0, 0))],
        out_specs=pl.BlockSpec((tm, n), lambda i: (i, 0)),
        out_shape=jax.ShapeDtypeStruct((m, n), F32),
        compiler_params=_params(1),
        name="gla_decay",
    )(h, w1p, w2p, b.reshape(1, n))


def _ml_gates_body(a_ref, w_ref, b_ref, o_ref, *, n_heads):
    z = _softcap(_dot(a_ref[...], w_ref[...].astype(BF16)) + b_ref[...])
    col = lax.broadcasted_iota(jnp.int32, z.shape, 1)
    o_ref[...] = jnp.where(col < n_heads, z, _log_sigmoid(z))


def _ml_gates(h, w_i, b_i, w_f, b_f, tok):
    m, d = h.shape
    nh = w_i.shape[1]
    pad = _LANES - 2 * nh
    w = jnp.pad(jnp.concatenate([w_i, w_f], axis=1), ((0, 0), (0, pad)))
    b = jnp.pad(jnp.concatenate([b_i, b_f]), (0, pad)).reshape(1, _LANES)
    tm = tok.tile
    return pl.pallas_call(
        functools.partial(_ml_gates_body, n_heads=nh),
        grid=(m // tm,),
        in_specs=[pl.BlockSpec((tm, d), lambda i: (i, 0)), pl.BlockSpec((d, _LANES), lambda i: (0, 0)),
                  pl.BlockSpec((1, _LANES), lambda i: (0, 0))],
        out_specs=pl.BlockSpec((tm, _LANES), lambda i: (i, 0)),
        out_shape=jax.ShapeDtypeStruct((m, _LANES), F32),
        compiler_params=_params(1),
        name="ml_gates",
    )(h, w, b)


def _causal(c):
    row = lax.broadcasted_iota(jnp.int32, (c, c), 0)
    col = lax.broadcasted_iota(jnp.int32, (c, c), 1)
    return row, col


def _seq_masks(c, ls):
    row = lax.broadcasted_iota(jnp.int32, (c, 1), 0)
    return [jnp.logical_and(row >= s * ls, row < (s + 1) * ls) for s in range(c // ls)]


def _head_norm(o):
    return o * lax.rsqrt(jnp.mean(o * o, axis=-1, keepdims=True) + EPS)


def _gla_chunk(q, k, v, e_b, e_nb, e_eb, decay_col, s):
    c = q.shape[0]
    row, col = _causal(c)
    q_dec = (q * e_b).astype(BF16)
    k_inv = (k * e_nb).astype(BF16)
    k_end = (k * e_eb).astype(BF16)
    vb = v.astype(BF16)
    att = jnp.where(row >= col, _dot_nt(q_dec, k_inv), 0.0)
    o = _dot(att.astype(BF16), vb) + _dot(q_dec, s.astype(BF16))
    s_new = decay_col * s + _dot_tn(k_end, vb)
    return o, s_new


def _gla_one(q, k, v, g, s):
    c = q.shape[0]
    row, col = _causal(c)
    tril = (row >= col).astype(BF16)
    g_hi = g.astype(BF16)
    g_lo = (g - g_hi.astype(F32)).astype(BF16)
    b = _dot(tril, g_hi) + _dot(tril, g_lo)
    b_end = b[c - 1:c, :]
    b_end_col = jnp.transpose(b[c - _SUBLANES:c, :])[:, _SUBLANES - 1:_SUBLANES]
    return _gla_chunk(q, k, v, jnp.exp(b), jnp.exp(-b), jnp.exp(b_end - b), jnp.exp(b_end_col), s)


def _gla_prompt_body(q_ref, k_ref, v_ref, la_ref, r_ref, go_ref, og_ref, s_ref, *, n_heads, dk, dv):
    @pl.when(pl.program_id(2) == 0)
    def _():
        s_ref[...] = jnp.zeros_like(s_ref)

    for hh in range(n_heads):
        sk = slice(hh * dk, (hh + 1) * dk)
        sv = slice(hh * dv, (hh + 1) * dv)
        o, s_new = _gla_one(q_ref[:, sk] * dk ** -0.5, k_ref[:, sk], v_ref[:, sv], la_ref[:, sk], s_ref[hh])
        s_ref[hh] = s_new
        og_ref[:, sv] = (_head_norm(o) * go_ref[...] * r_ref[:, sv]).astype(og_ref.dtype)


def _gla_sample_body(q_ref, k_ref, v_ref, la_ref, r_ref, go_ref, s0_ref, *rest, n_heads, dk, dv, ls):
    og_ref, s_ref = rest[-2], rest[-1]
    masks = _seq_masks(q_ref.shape[0], ls)
    for hh in range(n_heads):
        sk = slice(hh * dk, (hh + 1) * dk)
        sv = slice(hh * dv, (hh + 1) * dv)
        out = jnp.zeros((q_ref.shape[0], dv), F32)
        for bb, m in enumerate(masks):
            z = lambda t: jnp.where(m, t, 0.0)
            o, s_new = _gla_one(z(q_ref[:, sk]) * dk ** -0.5, z(k_ref[:, sk]), z(v_ref[:, sv]), z(la_ref[:, sk]),
                                s0_ref[bb, hh])
            s_ref[bb, hh] = s_new
            out = jnp.where(m, _head_norm(o) * go_ref[...], out)
        og_ref[:, sv] = out * r_ref[:, sv]


def _ret_one(q, k, v, steps, n_valid, lg, cos, sin, s):
    dk = q.shape[1]
    half = dk // 2

    def rot(t):
        t1, t2 = t[:, :half], t[:, half:]
        return jnp.concatenate([t1 * cos - t2 * sin, t1 * sin + t2 * cos], axis=-1)

    b = steps * lg
    b_end = jnp.full((1, 1), n_valid, F32) * lg
    return _gla_chunk(rot(q), rot(k) * dk ** -0.5, v, jnp.exp(b), jnp.exp(-b), jnp.exp(b_end - b), jnp.exp(b_end), s)


def _ret_prompt_body(lg_ref, q_ref, k_ref, v_ref, g_ref, cos_ref, sin_ref, og_ref, s_ref, *, n_heads, dk, dv):
    @pl.when(pl.program_id(2) == 0)
    def _():
        s_ref[...] = jnp.zeros_like(s_ref)

    c = q_ref.shape[0]
    steps = (lax.broadcasted_iota(jnp.int32, (c, 1), 0) + 1).astype(F32)
    head0 = pl.program_id(1) * n_heads
    for hh in range(n_heads):
        sk = slice(hh * dk, (hh + 1) * dk)
        sv = slice(hh * dv, (hh + 1) * dv)
        o, s_new = _ret_one(q_ref[:, sk], k_ref[:, sk], v_ref[:, sv], steps, c, lg_ref[head0 + hh],
                            cos_ref[...], sin_ref[...], s_ref[hh])
        s_ref[hh] = s_new
        og_ref[:, sv] = (g_ref[:, sv] * _head_norm(o)).astype(og_ref.dtype)


def _ret_sample_body(lg_ref, q_ref, k_ref, v_ref, g_ref, cos_ref, sin_ref, s0_ref, og_ref, s_ref, *,
                     head0, n_heads, dk, dv, ls):
    c = q_ref.shape[0]
    masks = _seq_masks(c, ls)
    row = lax.broadcasted_iota(jnp.int32, (c, 1), 0)
    for hh in range(n_heads):
        sk = slice(hh * dk, (hh + 1) * dk)
        sv = slice(hh * dv, (hh + 1) * dv)
        out = jnp.zeros((c, dv), F32)
        for bb, m in enumerate(masks):
            z = lambda t: jnp.where(m, t, 0.0)
            steps = jnp.clip(row - bb * ls + 1, 0, ls).astype(F32)
            o, s_new = _ret_one(z(q_ref[:, sk]), z(k_ref[:, sk]), z(v_ref[:, sv]), steps, ls, lg_ref[head0 + hh],
                                cos_ref[...], sin_ref[...], s0_ref[bb, hh])
            s_ref[bb, hh] = s_new
            out = jnp.where(m, _head_norm(o), out)
        og_ref[:, sv] = g_ref[:, sv] * out


def _ml_one(q, k, v, i_row, f_row, cm, n, m_prev):
    c = q.shape[0]
    row, col = _causal(c)
    causal = row >= col
    eye = row == col
    i_col = jnp.sum(jnp.where(eye, i_row, 0.0), axis=1, keepdims=True)
    f_col = jnp.sum(jnp.where(eye, f_row, 0.0), axis=1, keepdims=True)
    fcum_col = jnp.sum(jnp.where(causal, f_row, 0.0), axis=1, keepdims=True)
    fcum_row = jnp.sum(jnp.where(row <= col, f_col, 0.0), axis=0, keepdims=True)
    logw = jnp.where(causal, fcum_col - fcum_row + i_row, -jnp.inf)
    log_inter = fcum_col + m_prev
    m_t = jnp.maximum(log_inter, jnp.max(logw, axis=1, keepdims=True))
    w = jnp.exp(logw - m_t)
    a = jnp.exp(log_inter - m_t)
    qb = q.astype(BF16)
    vb = v.astype(BF16)
    s = _dot_nt(qb, k.astype(BF16)) * w
    num = _dot(s.astype(BF16), vb) + a * _dot(qb, cm.astype(BF16))
    den = jnp.sum(s, axis=1, keepdims=True) + a * jnp.sum(q * n, axis=1, keepdims=True)
    h = num / jnp.maximum(jnp.abs(den), jnp.exp(-m_t))
    m_new = m_t[c - 1:c, :]
    f_last = fcum_col[c - 1:c, :]
    a_end = jnp.exp(f_last + m_prev - m_new)
    kw = jnp.exp(f_last - fcum_col + i_col - m_new) * k
    c_new = a_end * cm + _dot_tn(kw.astype(BF16), vb)
    n_new = a_end * n + jnp.sum(kw, axis=0, keepdims=True)
    return h, c_new, n_new, m_new


def _ml_prompt_body(q_ref, k_ref, v_ref, og_ref, i_ref, f_ref, go_ref, out_ref, c_ref, n_ref, m_ref, *,
                    n_heads, dk, dv):
    @pl.when(pl.program_id(1) == 0)
    def _():
        c_ref[...] = jnp.zeros_like(c_ref)
        n_ref[...] = jnp.zeros_like(n_ref)
        m_ref[...] = jnp.zeros_like(m_ref)

    for hh in range(n_heads):
        sk = slice(hh * dk, (hh + 1) * dk)
        sv = slice(hh * dv, (hh + 1) * dv)
        h, c_new, n_new, m_new = _ml_one(q_ref[:, sk], k_ref[:, sk] * dk ** -0.5, v_ref[:, sv],
                                         i_ref[hh:hh + 1, :], f_ref[hh:hh + 1, :],
                                         c_ref[hh], n_ref[hh:hh + 1, :], m_ref[hh:hh + 1, :])
        c_ref[hh] = c_new
        n_ref[hh:hh + 1, :] = n_new
        m_ref[hh:hh + 1, :] = m_new
        out_ref[:, sv] = (og_ref[:, sv] * (_head_norm(h) * go_ref[...])).astype(out_ref.dtype)


def _ml_sample_body(q_ref, k_ref, v_ref, og_ref, i_ref, f_ref, go_ref, c0_ref, n0_ref, m0_ref,
                    out_ref, c_ref, n_ref, m_ref, *, n_heads, dk, dv, ls):
    rows = q_ref.shape[0]
    masks = _seq_masks(rows, ls)
    for hh in range(n_heads):
        sk = slice(hh * dk, (hh + 1) * dk)
        sv = slice(hh * dv, (hh + 1) * dv)
        out = jnp.zeros((rows, dv), F32)
        for bb, m in enumerate(masks):
            z = lambda t: jnp.where(m, t, 0.0)
            h, c_new, n_new, m_new = _ml_one(z(q_ref[:, sk]), z(k_ref[:, sk]) * dk ** -0.5, z(v_ref[:, sv]),
                                             i_ref[bb, hh:hh + 1, :], f_ref[bb, hh:hh + 1, :],
                                             c0_ref[bb, hh], n0_ref[bb, hh:hh + 1, :], m0_ref[bb, hh:hh + 1, :])
            c_ref[bb, hh] = c_new
            n_ref[bb, hh:hh + 1, :] = n_new
            m_ref[bb, hh:hh + 1, :] = m_new
            out = jnp.where(m, _head_norm(h) * go_ref[...], out)
        out_ref[:, sv] = og_ref[:, sv] * out


def _chunking(tok):
    c = CHUNK if tok.lp % CHUNK == 0 else tok.lp
    return c, tok.lp // c


def _pair_rows(tok):
    assert _SUBLANES % tok.ls == 0 and tok.mp % _SUBLANES == 0
    nb = _SUBLANES // tok.ls
    assert tok.bs % nb == 0
    return nb, tok.mp // _SUBLANES


def _sample_schedule(n_prompt_steps, pairs, nh, min_group):
    for gs in range(min_group, nh + 1):
        n_sample_steps = pairs * (nh // gs)
        if nh % gs == 0 and n_prompt_steps % n_sample_steps == 0:
            return gs, n_prompt_steps // n_sample_steps
    raise ValueError((n_prompt_steps, pairs, nh))


def _gla_fused_body(qp, kp, vp, lap, rp, go, qs, ks, vs, las, rs, s0, *rest, gp, gs, dk, dv, ls, ratio, nhg_p, nc):
    og_p, s_p, og_s, s_s = rest[-4:]
    _gla_prompt_body(qp, kp, vp, lap, rp, go, og_p, s_p, n_heads=gp, dk=dk, dv=dv)
    t = (pl.program_id(0) * nhg_p + pl.program_id(1)) * nc + pl.program_id(2)

    @pl.when(t % ratio == 0)
    def _():
        _gla_sample_body(qs, ks, vs, las, rs, go, s0, og_s, s_s, n_heads=gs, dk=dk, dv=dv, ls=ls)


def _gla_mix(q, k, v, la, r, g_o, state, layer, prev_states, tok, group_p):
    nl, bs, nh, dk, dv = state.shape
    c, nc = _chunking(tok)
    gp = group_p
    nhg_p = nh // gp
    nb, t0 = _pair_rows(tok)
    gs, ratio = _sample_schedule(tok.bp * nhg_p * nc, bs // nb, nh, 1)
    nhg_s = nh // gs

    def smp(b, hg, ci):
        u = ((b * nhg_p + hg) * nc + ci) // ratio
        return u // nhg_s, u % nhg_s

    def at_sample(f):
        return lambda b, hg, ci: f(*smp(b, hg, ci))

    tok_blk = lambda w: pl.BlockSpec((c, gp * w), lambda b, hg, ci: (b * nc + ci, hg))
    smp_blk = lambda w: pl.BlockSpec((_SUBLANES, gs * w), at_sample(lambda p, h: (t0 + p, h)))
    st_blk = pl.BlockSpec((None, nb, gs, dk, dv), at_sample(lambda p, h: (layer, p, h, 0, 0)))
    go = g_o.reshape(1, dv)
    args = [q, k, v, la, r, go, q, k, v, la, r, state]
    in_specs = [tok_blk(dk), tok_blk(dk), tok_blk(dv), tok_blk(dk), tok_blk(dv),
                pl.BlockSpec((1, dv), lambda b, hg, ci: (0, 0)),
                smp_blk(dk), smp_blk(dk), smp_blk(dv), smp_blk(dk), smp_blk(dv), st_blk]
    aliases = {}
    if prev_states is not None:
        args += list(prev_states)
        in_specs += [pl.BlockSpec(memory_space=pl.ANY)] * 2
        aliases = {len(args) - 2: 1, len(args) - 1: 3}
    og_p, s_p, og_s, s_s = pl.pallas_call(
        functools.partial(_gla_fused_body, gp=gp, gs=gs, dk=dk, dv=dv, ls=tok.ls, ratio=ratio, nhg_p=nhg_p, nc=nc),
        grid=(tok.bp, nhg_p, nc),
        in_specs=in_specs,
        out_specs=[tok_blk(dv), pl.BlockSpec((None, None, gp, dk, dv), lambda b, hg, ci: (layer, b, hg, 0, 0)),
                   pl.BlockSpec((_SUBLANES, gs * dv), at_sample(lambda p, h: (p, h))), st_blk],
        out_shape=(jax.ShapeDtypeStruct((tok.mp, nh * dv), BF16),
                   jax.ShapeDtypeStruct((nl, tok.bp, nh, dk, dv), F32),
                   jax.ShapeDtypeStruct((tok.ms, nh * dv), F32), jax.ShapeDtypeStruct(state.shape, F32)),
        input_output_aliases=aliases,
        compiler_params=_params(3),
        name="gla_mix",
    )(*args)
    return (og_p, og_s.astype(BF16)), (s_p, s_s)


def _rope_tables(pos, half):
    inv = ROPE_BASE ** (-jnp.arange(half, dtype=F32) / half)
    ang = pos[:, None] * inv[None, :]
    return jnp.cos(ang), jnp.sin(ang)


def _ret_fused_body(lg, qp, kp, vp, gp_ref, cosp, sinp, qs, ks, vs, gs_ref, coss, sins, s0, og_p, s_p, og_s, s_s, *,
                    gp, gs, dk, dv, ls, ratio, nhg_p, nhg_s, nc):
    _ret_prompt_body(lg, qp, kp, vp, gp_ref, cosp, sinp, og_p, s_p, n_heads=gp, dk=dk, dv=dv)
    t = (pl.program_id(0) * nhg_p + pl.program_id(1)) * nc + pl.program_id(2)

    @pl.when(t % ratio == 0)
    def _():
        _ret_sample_body(lg, qs, ks, vs, gs_ref, coss, sins, s0, og_s, s_s,
                         head0=((t // ratio) % nhg_s) * gs, n_heads=gs, dk=dk, dv=dv, ls=ls)


def _ret_mix(q, k, v, g, state, tok, group_p):
    _, bs, nh, dk, dv = state.shape
    c, nc = _chunking(tok)
    half = dk // 2
    log_gamma = jnp.log1p(-jnp.exp2(-5.0 - jnp.arange(nh, dtype=F32)))
    cos_p, sin_p = _rope_tables(0 + jnp.arange(tok.lp, dtype=F32), half)
    cos_s, sin_s = _rope_tables(PAST_LEN + jnp.arange(tok.ls, dtype=F32), half)
    gp = group_p
    nhg_p = nh // gp
    nb, t0 = _pair_rows(tok)
    gs, ratio = _sample_schedule(tok.bp * nhg_p * nc, bs // nb, nh, 1)
    nhg_s = nh // gs

    def at_sample(f):
        def index(b, hg, ci, lg):
            u = ((b * nhg_p + hg) * nc + ci) // ratio
            return f(u // nhg_s, u % nhg_s)
        return index

    tok_blk = lambda w: pl.BlockSpec((c, gp * w), lambda b, hg, ci, lg: (b * nc + ci, hg))
    rope_blk = pl.BlockSpec((c, half), lambda b, hg, ci, lg: (ci, 0))
    smp_blk = lambda w: pl.BlockSpec((_SUBLANES, gs * w), at_sample(lambda p, h: (t0 + p, h)))
    st_blk = pl.BlockSpec((None, nb, gs, dk, dv), at_sample(lambda p, h: (0, p, h, 0, 0)))
    rope_s = pl.BlockSpec((_SUBLANES, half), lambda b, hg, ci, lg: (0, 0))
    og_p, s_p, og_s, s_s = pl.pallas_call(
        functools.partial(_ret_fused_body, gp=gp, gs=gs, dk=dk, dv=dv, ls=tok.ls, ratio=ratio, nhg_p=nhg_p,
                          nhg_s=nhg_s, nc=nc),
        grid_spec=pltpu.PrefetchScalarGridSpec(
            num_scalar_prefetch=1,
            grid=(tok.bp, nhg_p, nc),
            in_specs=[tok_blk(dk), tok_blk(dk), tok_blk(dv), tok_blk(dv), rope_blk, rope_blk,
                      smp_blk(dk), smp_blk(dk), smp_blk(dv), smp_blk(dv), rope_s, rope_s, st_blk],
            out_specs=[tok_blk(dv), pl.BlockSpec((None, None, gp, dk, dv), lambda b, hg, ci, lg: (0, b, hg, 0, 0)),
                       pl.BlockSpec((_SUBLANES, gs * dv), at_sample(lambda p, h: (p, h))), st_blk],
        ),
        out_shape=(jax.ShapeDtypeStruct((tok.mp, nh * dv), BF16), jax.ShapeDtypeStruct((1, tok.bp, nh, dk, dv), F32),
                   jax.ShapeDtypeStruct((tok.ms, nh * dv), F32), jax.ShapeDtypeStruct(state.shape, F32)),
        compiler_params=_params(3),
        name="ret_mix",
    )(log_gamma, q, k, v, g, cos_p, sin_p, q, k, v, g, jnp.tile(cos_s, (nb, 1)), jnp.tile(sin_s, (nb, 1)), state)
    return (og_p, og_s.astype(BF16)), s_p, s_s


def _ml_mix(q, k, v, og, gates, g_o, c0, n0, m0, tok):
    _, bs, nh, dk, dv = c0.shape
    c, nc = _chunking(tok)
    nb, t0 = _pair_rows(tok)
    g_p = gates[:tok.mp, :2 * nh].reshape(tok.bp, nc, c, 2 * nh).transpose(0, 1, 3, 2)
    g_s = gates[tok.mp:, :2 * nh].reshape(bs, tok.ls, 2 * nh).transpose(0, 2, 1)
    own = (jnp.arange(nb)[None, :] == (jnp.arange(bs) % nb)[:, None])[:, None, :, None]
    i_s = jnp.where(own, g_s[:, :nh, None, :], -jnp.inf).reshape(bs, nh, _SUBLANES)
    f_s = jnp.where(own, g_s[:, nh:, None, :], 0.0).reshape(bs, nh, _SUBLANES)
    go = g_o.reshape(1, dv)

    tok_blk = lambda w: pl.BlockSpec((c, nh * w), lambda b, ci: (b * nc + ci, 0))
    gate_blk = lambda first: pl.BlockSpec((None, None, nh, c), lambda b, ci: (b, ci, first, 0))
    out_p, c_p, n_p, m_p = pl.pallas_call(
        functools.partial(_ml_prompt_body, n_heads=nh, dk=dk, dv=dv),
        grid=(tok.bp, nc),
        in_specs=[tok_blk(dk), tok_blk(dk), tok_blk(dv), tok_blk(dv), gate_blk(0), gate_blk(1),
                  pl.BlockSpec((1, dv), lambda b, ci: (0, 0))],
        out_specs=[tok_blk(dv),
                   pl.BlockSpec((None, None, nh, dk, dv), lambda b, ci: (0, b, 0, 0, 0)),
                   pl.BlockSpec((None, None, nh, dk), lambda b, ci: (0, b, 0, 0)),
                   pl.BlockSpec((None, None, nh, 1), lambda b, ci: (0, b, 0, 0))],
        out_shape=(jax.ShapeDtypeStruct((tok.mp, nh * dv), BF16),
                   jax.ShapeDtypeStruct((1, tok.bp, nh, dk, dv), F32),
                   jax.ShapeDtypeStruct((1, tok.bp, nh, dk), F32),
                   jax.ShapeDtypeStruct((1, tok.bp, nh, 1), F32)),
        compiler_params=_params(2),
        name="ml_prompt",
    )(q, k, v, og, g_p, g_p, go)

    smp_blk = lambda w: pl.BlockSpec((_SUBLANES, nh * w), lambda b: (t0 + b, 0))
    sgate_blk = pl.BlockSpec((nb, nh, _SUBLANES), lambda b: (b, 0, 0))
    c_blk = pl.BlockSpec((None, nb, nh, dk, dv), lambda b: (0, b, 0, 0, 0))
    n_blk = pl.BlockSpec((None, nb, nh, dk), lambda b: (0, b, 0, 0))
    m_blk = pl.BlockSpec((None, nb, nh, 1), lambda b: (0, b, 0, 0))
    out_s, c_s, n_s, m_s = pl.pallas_call(
        functools.partial(_ml_sample_body, n_heads=nh, dk=dk, dv=dv, ls=tok.ls),
        grid=(bs // nb,),
        in_specs=[smp_blk(dk), smp_blk(dk), smp_blk(dv), smp_blk(dv), sgate_blk, sgate_blk,
                  pl.BlockSpec((1, dv), lambda b: (0, 0)), c_blk, n_blk, m_blk],
        out_specs=[pl.BlockSpec((_SUBLANES, nh * dv), lambda b: (b, 0)), c_blk, n_blk, m_blk],
        out_shape=(jax.ShapeDtypeStruct((tok.ms, nh * dv), F32),
                   jax.ShapeDtypeStruct(c0.shape, F32), jax.ShapeDtypeStruct(n0.shape, F32),
                   jax.ShapeDtypeStruct(m0.shape + (1,), F32)),
        compiler_params=_params(1),
        name="ml_sample",
    )(q, k, v, og, i_s, f_s, go, c0, n0, m0[..., None])
    return (out_p, out_s.astype(BF16)), (c_p, n_p, m_p[..., 0]), (c_s, n_s, m_s[..., 0])


def kernel(x_prompt, x_sample, state_gla, state_ret, state_mlstm_C, state_mlstm_n, state_mlstm_m, c_prompt, c_sample, w_ada, b_ada, g_mix, g_mlp, w_up, w_down, g_final, w_gla_q, w_gla_k, w_gla_v, w_gla_a1, w_gla_a2, b_gla_a, w_gla_r, b_gla_r, g_gla_o, w_gla_o, w_ret_q, w_ret_k, w_ret_v, w_ret_g, w_ret_o, w_ml_q, w_ml_k, w_ml_v, w_ml_i, b_ml_i, w_ml_f, b_ml_f, w_ml_og, g_ml_o, w_ml_o):
    bp, lp, d = x_prompt.shape
    bs, ls, _ = x_sample.shape
    depth = w_ada.shape[0]
    ms = bs * ls
    tok = _Tok(bp, lp, bs, ls, min(_MM_TILE, ms))
    ntok = _Tok(bp, lp, bs, ls, min(_NORM_TILE, ms))
    tn = min(_MM_TN, d)

    x = (x_prompt.reshape(bp * lp, d), x_sample.reshape(ms, d))

    n_c = bs + bp
    rows = -(-n_c // 16) * 16
    c_all = jnp.pad(jnp.concatenate([c_sample, c_prompt], axis=0), ((0, rows - n_c), (0, 0)))
    mod = _Mod(_ada_mod(c_all, w_ada, b_ada, bs, ls, min(_ADA_TN, d)), bp, ms, d)

    w_down_last = _last_chunk_bf16(w_down, d)
    w_o_last = [_last_chunk_bf16(w, d) for w in (w_gla_o, w_ret_o, w_ml_o)]

    gla_states = None
    ret_p = ret_s = ml_p = ml_s = None
    for i in range(depth):
        kind, j = i % 3, i // 3
        h = _norm_mod(x, g_mix[i], mod, i, 1, 0, ntok)
        if kind == 0:
            q = _mm(h, w_gla_q, j, tok, tn=tn, name="gla_q")
            k = _mm(h, w_gla_k, j, tok, tn=tn, name="gla_k")
            v = _mm(h, w_gla_v, j, tok, tn=tn, name="gla_v")
            r = _mm(h, w_gla_r, j, tok, tn=tn, epi=_epi_bias_silu, extras=_bias_extras(b_gla_r[j], tn), name="gla_r")
            la = _gla_decay(h, w_gla_a1[j], w_gla_a2[j], b_gla_a[j], tok)
            nh = state_gla.shape[2]
            og, gla_states = _gla_mix(q, k, v, la, r, g_gla_o[j], state_gla, j, gla_states, tok, min(2, nh))
            w_o = w_gla_o
        elif kind == 1:
            q = _mm(h, w_ret_q, j, tok, tn=tn, name="ret_q")
            k = _mm(h, w_ret_k, j, tok, tn=tn, name="ret_k")
            v = _mm(h, w_ret_v, j, tok, tn=tn, name="ret_v")
            g = _mm(h, w_ret_g, j, tok, tn=tn, epi=_epi_silu, name="ret_g")
            nh = state_ret.shape[2]
            og, ret_p, ret_s = _ret_mix(q, k, v, g, state_ret, tok, min(8, nh))
            w_o = w_ret_o
        else:
            q = _mm(h, w_ml_q, j, tok, tn=tn, name="ml_q")
            k = _mm(h, w_ml_k, j, tok, tn=tn, name="ml_k")
            v = _mm(h, w_ml_v, j, tok, tn=tn, name="ml_v")
            ogate = _mm(h, w_ml_og, j, tok, tn=tn, epi=_epi_sigmoid, name="ml_og")
            gates = _ml_gates(h, w_ml_i[j], b_ml_i[j], w_ml_f[j], b_ml_f[j], tok)
            og, ml_p, ml_s = _ml_mix(q, k, v, ogate, gates, g_ml_o[j], state_mlstm_C, state_mlstm_n,
                                     state_mlstm_m, tok)
            w_o = w_ml_o
        x = _mm_resid(og, w_o, w_o_last[kind], j, x, mod, i, 2, tok, d, name="mix_o")
        hf = _norm_mod(x, g_mlp[i], mod, i, 4, 3, ntok)
        hid = _mm(hf, w_up, i, tok, tn=tn, epi=_epi_relu2, out_dtype=BF16, name="mlp_up")
        x = _mm_resid(hid, w_down, w_down_last, i, x, mod, i, 5, tok, d, name="mlp_down")

    y_p = _final_norm(x, g_final, ntok.tile, 0, ntok.npt).reshape(bp, lp, d)
    y_s = _final_norm(x, g_final, ntok.tile, ntok.npt, ntok.nst).reshape(bs, ls, d)
    return (y_p, y_s, gla_states[0], gla_states[1], ret_p, ret_s,
            ml_p[0], ml_s[0], ml_p[1], ml_s[1], ml_p[2], ml_s[2])
```

```python
import functools

import jax
import jax.numpy as jnp
from jax import lax
from jax.experimental import pallas as pl
from jax.experimental.pallas import tpu as pltpu

F32 = jnp.float32
BF16 = jnp.bfloat16

CHUNK = 64
EPS = 1e-6
GLA_TAU = 16.0
ML_GATE_CAP = 15.0
ROPE_BASE = 10000.0
PAST_LEN = 16384

_VMEM_LIMIT_V7X = 60 * 2**20
_LANES = 128
_SUBLANES = 8
_MM_TILE = 512
_NORM_TILE = 256
_MM_TN = 1024
_RESID_TN = 512
_ADA_TN = 512


def _params(n_grid, limit=_VMEM_LIMIT_V7X):
    return pltpu.CompilerParams(dimension_semantics=("arbitrary",) * n_grid, vmem_limit_bytes=limit)


def _dot(a, b):
    return jnp.dot(a, b, preferred_element_type=F32)


def _dot_nt(a, b):
    return lax.dot_general(a, b, (((1,), (1,)), ((), ())), preferred_element_type=F32)


def _dot_tn(a, b):
    return lax.dot_general(a, b, (((0,), (0,)), ((), ())), preferred_element_type=F32)


def _sigmoid(x):
    return 1.0 / (1.0 + jnp.exp(-x))


def _silu(x):
    return x * _sigmoid(x)


def _log_sigmoid(x):
    return jnp.minimum(x, 0.0) - jnp.log1p(jnp.exp(-jnp.abs(x)))


def _softcap(z):
    return ML_GATE_CAP * jnp.tanh(z / ML_GATE_CAP)


class _Tok:
    def __init__(self, bp, lp, bs, ls, tile):
        self.bp, self.lp, self.bs, self.ls = bp, lp, bs, ls
        self.mp, self.ms = bp * lp, bs * ls
        self.m = self.mp + self.ms
        assert lp % tile == 0 and self.ms % tile == 0, (lp, self.ms, tile)
        self.tile = tile
        self.npt = self.mp // tile
        self.nst = self.ms // tile
        self.tpb = lp // tile

    def prompt_idx(self, i):
        return jnp.minimum(i // self.tpb, self.bp - 1)

    def sample_idx(self, i):
        return jnp.maximum(i - self.npt, 0)


class _Mod:
    def __init__(self, mod, bp, ms, d):
        depth = mod.shape[0]
        self.p = mod[:, ms:ms + bp].reshape(depth, bp, 1, mod.shape[2])
        self.s = mod
        self.d = d

    def specs(self, layer, chunk, tok, width, col):
        per = self.d // width
        return [
            (self.p, (None, None, 1, width), lambda i, j: (layer, tok.prompt_idx(i), 0, chunk * per + col(j))),
            (self.s, (None, tok.tile, width), lambda i, j: (layer, tok.sample_idx(i), chunk * per + col(j))),
        ]


def _ada_body(c_ref, w_ref, b_ref, o_ref, sc_ref, *, bs, ls):
    @pl.when(jnp.logical_and(pl.program_id(0) == 0, pl.program_id(1) == 0))
    def _():
        sc_ref[...] = _silu(c_ref[...]).astype(BF16)

    res = _dot(sc_ref[...], w_ref[...].astype(BF16)) + b_ref[...]
    ms = bs * ls
    rs = res[:bs]
    hi = rs.astype(BF16)
    r1 = rs - hi.astype(F32)
    mid = r1.astype(BF16)
    lo = (r1 - mid.astype(F32)).astype(BF16)
    row = lax.broadcasted_iota(jnp.int32, (ms, bs), 0)
    col = lax.broadcasted_iota(jnp.int32, (ms, bs), 1)
    sel = jnp.logical_and(row >= col * ls, row < (col + 1) * ls).astype(BF16)
    o_ref[:ms, :] = (_dot(sel, hi) + _dot(sel, mid)) + _dot(sel, lo)
    o_ref[ms:, :] = res[bs:]


def _ada_mod(c_all, w_ada, b_ada, bs, ls, tn):
    depth, d, n = w_ada.shape
    rows = c_all.shape[0]
    rows_out = bs * ls + rows - bs
    return pl.pallas_call(
        functools.partial(_ada_body, bs=bs, ls=ls),
        grid=(depth, n // tn),
        in_specs=[
            pl.BlockSpec((rows, d), lambda l, j: (0, 0)),
            pl.BlockSpec((None, d, tn), lambda l, j: (l, 0, j)),
            pl.BlockSpec((None, 1, tn), lambda l, j: (l, 0, j)),
        ],
        out_specs=pl.BlockSpec((None, rows_out, tn), lambda l, j: (l, 0, j)),
        out_shape=jax.ShapeDtypeStruct((depth, rows_out, n), F32),
        scratch_shapes=[pltpu.VMEM((rows, d), BF16)],
        compiler_params=_params(2),
        name="ada_mod",
    )(c_all, w_ada, b_ada.reshape(depth, 1, n))


def _rms(x, g):
    return x * lax.rsqrt(jnp.mean(x * x, axis=-1, keepdims=True) + EPS) * g


def _norm_mod_body(*refs, npt, n_x):
    xs = refs[:n_x]
    g_ref, scp_ref, scs_ref, shp_ref, shs_ref, o_ref = refs[n_x:]
    i = pl.program_id(0)

    @pl.when(i < npt)
    def _():
        y = _rms(xs[0][...], g_ref[...])
        o_ref[...] = (y * (1.0 + scp_ref[...]) + shp_ref[...]).astype(o_ref.dtype)

    @pl.when(i >= npt)
    def _():
        y = _rms(xs[-1][...], g_ref[...])
        o_ref[...] = (y * (1.0 + scs_ref[...]) + shs_ref[...]).astype(o_ref.dtype)


def _row_specs(x, tok, width, col):
    if isinstance(x, tuple):
        return [(x[0], (tok.tile, width), lambda i, j: (jnp.minimum(i, tok.npt - 1), col(j))),
                (x[1], (tok.tile, width), lambda i, j: (tok.sample_idx(i), col(j)))]
    return [(x, (tok.tile, width), lambda i, j: (i, col(j)))]


def _norm_mod(x, g, mod, layer, scale_chunk, shift_chunk, tok):
    d = g.shape[0]
    t = tok.tile
    xr = _row_specs(x, tok, d, lambda j: 0)
    ex = xr + [(g.reshape(1, d), (1, d), lambda i, j: (0, 0))]
    ex += mod.specs(layer, scale_chunk, tok, d, lambda j: 0) + mod.specs(layer, shift_chunk, tok, d, lambda j: 0)
    return pl.pallas_call(
        functools.partial(_norm_mod_body, npt=tok.npt, n_x=len(xr)),
        grid=(tok.m // t,),
        in_specs=[pl.BlockSpec(blk, functools.partial(lambda f, i: f(i, 0), f)) for _, blk, f in ex],
        out_specs=pl.BlockSpec((t, d), lambda i: (i, 0)),
        out_shape=jax.ShapeDtypeStruct((tok.m, d), BF16),
        compiler_params=_params(1),
        name="norm_mod",
    )(*[e[0] for e in ex])


def _final_norm_body(x_ref, g_ref, o_ref):
    o_ref[...] = _rms(x_ref[...], g_ref[...])


def _final_norm(x, g, tile, first_tile, n_tiles):
    d = x.shape[1]
    return pl.pallas_call(
        _final_norm_body,
        grid=(n_tiles,),
        in_specs=[pl.BlockSpec((tile, d), lambda i: (i + first_tile, 0)), pl.BlockSpec((1, d), lambda i: (0, 0))],
        out_specs=pl.BlockSpec((tile, d), lambda i: (i, 0)),
        out_shape=jax.ShapeDtypeStruct((n_tiles * tile, d), F32),
        compiler_params=_params(1),
        name="final_norm",
    )(x, g.reshape(1, d))


def _epi_plain(acc, ex, is_prompt):
    return acc


def _epi_bias_silu(acc, ex, is_prompt):
    return _silu(acc + ex[0][...])


def _epi_silu(acc, ex, is_prompt):
    return _silu(acc)


def _epi_sigmoid(acc, ex, is_prompt):
    return _sigmoid(acc)


def _epi_relu2(acc, ex, is_prompt):
    r = jnp.maximum(acc, 0.0)
    return r * r


def _epi_add(acc, ex, is_prompt):
    return ex[0][...] + acc


def _epi_resid(n_x, has_part):
    def epi(acc, ex, is_prompt):
        gp_ref, gs_ref = ex[n_x:n_x + 2]
        if has_part:
            acc = ex[n_x + 2][...] + acc
        x = ex[0][...] if n_x == 1 else jnp.where(is_prompt, ex[0][...], ex[1][...])
        return x + jnp.where(is_prompt, gp_ref[...], gs_ref[...]) * acc

    return epi


def _mm_body(*refs, epi, n_act, n_ex, npt, cast_w):
    acts, w_ref = refs[:n_act], refs[n_act]
    ex = refs[n_act + 1:n_act + 1 + n_ex]
    o_ref = refs[n_act + 1 + n_ex]
    i = pl.program_id(1)
    wbf_ref = w_ref
    if cast_w:
        wbf_ref = refs[n_act + 2 + n_ex]

        @pl.when(i == 0)
        def _():
            wbf_ref[...] = w_ref[...].astype(BF16)

    def run(a_ref):
        acc = _dot(a_ref[...], wbf_ref[...])
        o_ref[...] = epi(acc, ex, i < npt).astype(o_ref.dtype)

    if n_act == 1:
        run(acts[0])
    else:
        pl.when(i < npt)(lambda: run(acts[0]))
        pl.when(i >= npt)(lambda: run(acts[1]))


def _mm(a, w, layer, tok, *, tn, tk=None, kb=0, w_kb=None, epi=_epi_plain, extras=(), out_dtype=F32, name="mm"):
    acts = a if isinstance(a, tuple) else (a,)
    kdim = acts[0].shape[1]
    n = w.shape[2]
    tm = tok.tile
    tn = min(tn, n)
    tk = kdim if tk is None else tk
    w_kb = kb if w_kb is None else w_kb
    cast_w = w.dtype != BF16
    assert n % tn == 0 and kdim % tk == 0 and w.shape[1] % tk == 0
    if len(acts) == 1:
        a_specs = [pl.BlockSpec((tm, tk), lambda j, i: (i, kb))]
    else:
        a_specs = [pl.BlockSpec((tm, tk), lambda j, i: (jnp.minimum(i, tok.npt - 1), kb)),
                   pl.BlockSpec((tm, tk), lambda j, i: (tok.sample_idx(i), kb))]
    ex_specs = [pl.BlockSpec(blk, functools.partial(lambda f, j, i: f(i, j), f)) for _, blk, f in extras]
    return pl.pallas_call(
        functools.partial(_mm_body, epi=epi, n_act=len(acts), n_ex=len(extras), npt=tok.npt, cast_w=cast_w),
        grid=(n // tn, tok.m // tm),
        in_specs=a_specs + [pl.BlockSpec((None, tk, tn), lambda j, i: (layer, w_kb, j))] + ex_specs,
        out_specs=pl.BlockSpec((tm, tn), lambda j, i: (i, j)),
        out_shape=jax.ShapeDtypeStruct((tok.m, n), out_dtype),
        scratch_shapes=[pltpu.VMEM((tk, tn), BF16)] if cast_w else [],
        compiler_params=_params(2),
        name=name,
    )(*acts, w, *[e[0] for e in extras])


def _tile_extra(t, tok, tn):
    return (t, (tok.tile, tn), lambda i, j: (i, j))


def _last_chunk_bf16(w, d):
    return w[:, w.shape[1] - d:, :].astype(BF16)


def _mm_resid(a, w, w_last, layer, x, mod, mod_layer, gate_chunk, tok, d, name):
    acts = a if isinstance(a, tuple) else (a,)
    kdim = acts[0].shape[1]
    nk = -(-kdim // d)
    tk = kdim // nk
    assert w_last.shape[1] == tk
    part = None
    for kb in range(nk - 1):
        tn = _MM_TN if len(acts) == 1 else _RESID_TN
        tn = min(tn, w.shape[2])
        ex = [] if part is None else [_tile_extra(part, tok, tn)]
        part = _mm(a, w, layer, tok, tn=tn, tk=tk, kb=kb, epi=_epi_plain if part is None else _epi_add, extras=ex,
                   name=name + "_part")
    tn = min(_MM_TN, w.shape[2])
    xr = _row_specs(x, tok, tn, lambda j: j)
    ex = xr + mod.specs(mod_layer, gate_chunk, tok, tn, lambda j: j)
    if part is not None:
        ex.append(_tile_extra(part, tok, tn))
    return _mm(a, w_last, layer, tok, tn=tn, tk=tk, kb=nk - 1, w_kb=0, epi=_epi_resid(len(xr), part is not None),
               extras=ex, name=name)


def _bias_extras(b, tn):
    return [(b.reshape(1, -1), (1, tn), lambda i, j: (0, j))]


def _gla_decay_body(a_ref, w1_ref, w2_ref, b_ref, o_ref):
    t = _dot(a_ref[...], w1_ref[...].astype(BF16)).astype(BF16)
    z = _dot(t, w2_ref[...].astype(BF16)) + b_ref[...]
    o_ref[...] = _log_sigmoid(z) / GLA_TAU


def _gla_decay(h, w1, w2, b, tok):
    m, d = h.shape
    rank, n = w2.shape
    pad = _LANES - rank
    w1p = jnp.pad(w1, ((0, 0), (0, pad)))
    w2p = jnp.pad(w2, ((0, pad), (0, 0)))
    tm = tok.tile
    return pl.pallas_call(
        _gla_decay_body,
        grid=(m // tm,),
        in_specs=[pl.BlockSpec((tm, d), lambda i: (i, 0)), pl.BlockSpec((d, _LANES), lambda i: (0, 0)),
                  pl.BlockSpec((_LANES, n), lambda i: (0, 0)), pl.BlockSpec((1, n), lambda i: (0, 0))],
        out_specs=pl.BlockSpec((tm, n), lambda i: (i, 0)),
        out_shape=jax.ShapeDtypeStruct((m, n), F32),
        compiler_params=_params(1),
        name="gla_decay",
    )(h, w1p, w2p, b.reshape(1, n))


def _ml_gates_body(a_ref, w_ref, b_ref, o_ref, *, n_heads):
    z = _softcap(_dot(a_ref[...], w_ref[...].astype(BF16)) + b_ref[...])
    col = lax.broadcasted_iota(jnp.int32, z.shape, 1)
    o_ref[...] = jnp.where(col < n_heads, z, _log_sigmoid(z))


def _ml_gates(h, w_i, b_i, w_f, b_f, tok):
    m, d = h.shape
    nh = w_i.shape[1]
    pad = _LANES - 2 * nh
    w = jnp.pad(jnp.concatenate([w_i, w_f], axis=1), ((0, 0), (0, pad)))
    b = jnp.pad(jnp.concatenate([b_i, b_f]), (0, pad)).reshape(1, _LANES)
    tm = tok.tile
    return pl.pallas_call(
        functools.partial(_ml_gates_body, n_heads=nh),
        grid=(m // tm,),
        in_specs=[pl.BlockSpec((tm, d), lambda i: (i, 0)), pl.BlockSpec((d, _LANES), lambda i: (0, 0)),
                  pl.BlockSpec((1, _LANES), lambda i: (0, 0))],
        out_specs=pl.BlockSpec((tm, _LANES), lambda i: (i, 0)),
        out_shape=jax.ShapeDtypeStruct((m, _LANES), F32),
        compiler_params=_params(1),
        name="ml_gates",
    )(h, w, b)


def _causal(c):
    row = lax.broadcasted_iota(jnp.int32, (c, c), 0)
    col = lax.broadcasted_iota(jnp.int32, (c, c), 1)
    return row, col


def _seq_masks(c, ls):
    row = lax.broadcasted_iota(jnp.int32, (c, 1), 0)
    return [jnp.logical_and(row >= s * ls, row < (s + 1) * ls) for s in range(c // ls)]


def _head_norm(o):
    return o * lax.rsqrt(jnp.mean(o * o, axis=-1, keepdims=True) + EPS)


def _gla_chunk(q, k, v, e_b, e_nb, e_eb, decay_col, s):
    c = q.shape[0]
    row, col = _causal(c)
    q_dec = (q * e_b).astype(BF16)
    k_inv = (k * e_nb).astype(BF16)
    k_end = (k * e_eb).astype(BF16)
    vb = v.astype(BF16)
    att = jnp.where(row >= col, _dot_nt(q_dec, k_inv), 0.0)
    o = _dot(att.astype(BF16), vb) + _dot(q_dec, s.astype(BF16))
    s_new = decay_col * s + _dot_tn(k_end, vb)
    return o, s_new


def _gla_one(q, k, v, g, s):
    c = q.shape[0]
    row, col = _causal(c)
    tril = (row >= col).astype(BF16)
    g_hi = g.astype(BF16)
    g_lo = (g - g_hi.astype(F32)).astype(BF16)
    b = _dot(tril, g_hi) + _dot(tril, g_lo)
    b_end = b[c - 1:c, :]
    b_end_col = jnp.transpose(b[c - _SUBLANES:c, :])[:, _SUBLANES - 1:_SUBLANES]
    return _gla_chunk(q, k, v, jnp.exp(b), jnp.exp(-b), jnp.exp(b_end - b), jnp.exp(b_end_col), s)


def _gla_prompt_body(q_ref, k_ref, v_ref, la_ref, r_ref, go_ref, og_ref, s_ref, *, n_heads, dk, dv):
    @pl.when(pl.program_id(2) == 0)
    def _():
        s_ref[...] = jnp.zeros_like(s_ref)

    c = q_ref.shape[0]
    row, col = _causal(c)
    causal = row >= col
    tril = causal.astype(BF16)
    heads = range(n_heads)
    sk = [slice(h * dk, (h + 1) * dk) for h in heads]
    sv = [slice(h * dv, (h + 1) * dv) for h in heads]
    g = [la_ref[:, sk[h]] for h in heads]
    g_hi = [x.astype(BF16) for x in g]
    g_lo = [(g[h] - g_hi[h].astype(F32)).astype(BF16) for h in heads]
    b = [_dot(tril, g_hi[h]) + _dot(tril, g_lo[h]) for h in heads]
    b_end = [x[c - 1:c, :] for x in b]
    decay = [jnp.exp(jnp.transpose(x[c - _SUBLANES:c, :])[:, _SUBLANES - 1:_SUBLANES]) for x in b]
    q_dec = [(q_ref[:, sk[h]] * dk ** -0.5 * jnp.exp(b[h])).astype(BF16) for h in heads]
    k_inv = [(k_ref[:, sk[h]] * jnp.exp(-b[h])).astype(BF16) for h in heads]
    k_end = [(k_ref[:, sk[h]] * jnp.exp(b_end[h] - b[h])).astype(BF16) for h in heads]
    vb = [v_ref[:, sv[h]].astype(BF16) for h in heads]
    att = [jnp.where(causal, _dot_nt(q_dec[h], k_inv[h]), 0.0).astype(BF16) for h in heads]
    s = [s_ref[h] for h in heads]
    o = [_dot(att[h], vb[h]) + _dot(q_dec[h], s[h].astype(BF16)) for h in heads]
    s_new = [decay[h] * s[h] + _dot_tn(k_end[h], vb[h]) for h in heads]
    for h in heads:
        s_ref[h] = s_new[h]
        og_ref[:, sv[h]] = (_head_norm(o[h]) * go_ref[...] * r_ref[:, sv[h]]).astype(og_ref.dtype)


def _gla_sample_body(q_ref, k_ref, v_ref, la_ref, r_ref, go_ref, s0_ref, *rest, n_heads, dk, dv, ls):
    og_ref, s_ref = rest[-2], rest[-1]
    masks = _seq_masks(q_ref.shape[0], ls)
    for hh in range(n_heads):
        sk = slice(hh * dk, (hh + 1) * dk)
        sv = slice(hh * dv, (hh + 1) * dv)
        out = jnp.zeros((q_ref.shape[0], dv), F32)
        for bb, m in enumerate(masks):
            z = lambda t: jnp.where(m, t, 0.0)
            o, s_new = _gla_one(z(q_ref[:, sk]) * dk ** -0.5, z(k_ref[:, sk]), z(v_ref[:, sv]), z(la_ref[:, sk]),
                                s0_ref[bb, hh])
            s_ref[bb, hh] = s_new
            out = jnp.where(m, _head_norm(o) * go_ref[...], out)
        og_ref[:, sv] = out * r_ref[:, sv]


def _ret_one(q, k, v, steps, n_valid, lg, cos, sin, s):
    dk = q.shape[1]
    half = dk // 2

    def rot(t):
        t1, t2 = t[:, :half], t[:, half:]
        return jnp.concatenate([t1 * cos - t2 * sin, t1 * sin + t2 * cos], axis=-1)

    b = steps * lg
    b_end = jnp.full((1, 1), n_valid, F32) * lg
    return _gla_chunk(rot(q), rot(k) * dk ** -0.5, v, jnp.exp(b), jnp.exp(-b), jnp.exp(b_end - b), jnp.exp(b_end), s)


def _ret_prompt_body(lg_ref, q_ref, k_ref, v_ref, g_ref, cos_ref, sin_ref, og_ref, s_ref, *, n_heads, dk, dv):
    @pl.when(pl.program_id(2) == 0)
    def _():
        s_ref[...] = jnp.zeros_like(s_ref)

    c = q_ref.shape[0]
    steps = (lax.broadcasted_iota(jnp.int32, (c, 1), 0) + 1).astype(F32)
    head0 = pl.program_id(1) * n_heads
    for hh in range(n_heads):
        sk = slice(hh * dk, (hh + 1) * dk)
        sv = slice(hh * dv, (hh + 1) * dv)
        o, s_new = _ret_one(q_ref[:, sk], k_ref[:, sk], v_ref[:, sv], steps, c, lg_ref[head0 + hh],
                            cos_ref[...], sin_ref[...], s_ref[hh])
        s_ref[hh] = s_new
        og_ref[:, sv] = (g_ref[:, sv] * _head_norm(o)).astype(og_ref.dtype)


def _ret_sample_body(lg_ref, q_ref, k_ref, v_ref, g_ref, cos_ref, sin_ref, s0_ref, og_ref, s_ref, *,
                     head0, n_heads, dk, dv, ls):
    c = q_ref.shape[0]
    masks = _seq_masks(c, ls)
    row = lax.broadcasted_iota(jnp.int32, (c, 1), 0)
    for hh in range(n_heads):
        sk = slice(hh * dk, (hh + 1) * dk)
        sv = slice(hh * dv, (hh + 1) * dv)
        out = jnp.zeros((c, dv), F32)
        for bb, m in enumerate(masks):
            z = lambda t: jnp.where(m, t, 0.0)
            steps = jnp.clip(row - bb * ls + 1, 0, ls).astype(F32)
            o, s_new = _ret_one(z(q_ref[:, sk]), z(k_ref[:, sk]), z(v_ref[:, sv]), steps, ls, lg_ref[head0 + hh],
                                cos_ref[...], sin_ref[...], s0_ref[bb, hh])
            s_ref[bb, hh] = s_new
            out = jnp.where(m, _head_norm(o), out)
        og_ref[:, sv] = g_ref[:, sv] * out


def _ml_one(q, k, v, i_row, f_row, cm, n, m_prev):
    c = q.shape[0]
    row, col = _causal(c)
    causal = row >= col
    eye = row == col
    i_col = jnp.sum(jnp.where(eye, i_row, 0.0), axis=1, keepdims=True)
    f_col = jnp.sum(jnp.where(eye, f_row, 0.0), axis=1, keepdims=True)
    fcum_col = jnp.sum(jnp.where(causal, f_row, 0.0), axis=1, keepdims=True)
    fcum_row = jnp.sum(jnp.where(row <= col, f_col, 0.0), axis=0, keepdims=True)
    logw = jnp.where(causal, fcum_col - fcum_row + i_row, -jnp.inf)
    log_inter = fcum_col + m_prev
    m_t = jnp.maximum(log_inter, jnp.max(logw, axis=1, keepdims=True))
    w = jnp.exp(logw - m_t)
    a = jnp.exp(log_inter - m_t)
    qb = q.astype(BF16)
    vb = v.astype(BF16)
    s = _dot_nt(qb, k.astype(BF16)) * w
    num = _dot(s.astype(BF16), vb) + a * _dot(qb, cm.astype(BF16))
    den = jnp.sum(s, axis=1, keepdims=True) + a * jnp.sum(q * n, axis=1, keepdims=True)
    h = num / jnp.maximum(jnp.abs(den), jnp.exp(-m_t))
    m_new = m_t[c - 1:c, :]
    f_last = fcum_col[c - 1:c, :]
    a_end = jnp.exp(f_last + m_prev - m_new)
    kw = jnp.exp(f_last - fcum_col + i_col - m_new) * k
    c_new = a_end * cm + _dot_tn(kw.astype(BF16), vb)
    n_new = a_end * n + jnp.sum(kw, axis=0, keepdims=True)
    return h, c_new, n_new, m_new


def _ml_prompt_body(q_ref, k_ref, v_ref, og_ref, i_ref, f_ref, go_ref, out_ref, c_ref, n_ref, m_ref, *,
                    n_heads, dk, dv):
    @pl.when(pl.program_id(1) == 0)
    def _():
        c_ref[...] = jnp.zeros_like(c_ref)
        n_ref[...] = jnp.zeros_like(n_ref)
        m_ref[...] = jnp.zeros_like(m_ref)

    c = q_ref.shape[0]
    row, col = _causal(c)
    causal = row >= col
    eye = row == col
    heads = range(n_heads)
    sk = [slice(h * dk, (h + 1) * dk) for h in heads]
    sv = [slice(h * dv, (h + 1) * dv) for h in heads]
    i_row = [i_ref[h:h + 1, :] for h in heads]
    f_row = [f_ref[h:h + 1, :] for h in heads]
    m_prev = [m_ref[h:h + 1, :] for h in heads]
    i_col = [jnp.sum(jnp.where(eye, i_row[h], 0.0), axis=1, keepdims=True) for h in heads]
    f_col = [jnp.sum(jnp.where(eye, f_row[h], 0.0), axis=1, keepdims=True) for h in heads]
    fcum_col = [jnp.sum(jnp.where(causal, f_row[h], 0.0), axis=1, keepdims=True) for h in heads]
    fcum_row = [jnp.sum(jnp.where(row <= col, f_col[h], 0.0), axis=0, keepdims=True) for h in heads]
    logw = [jnp.where(causal, fcum_col[h] - fcum_row[h] + i_row[h], -jnp.inf) for h in heads]
    log_inter = [fcum_col[h] + m_prev[h] for h in heads]
    m_t = [jnp.maximum(log_inter[h], jnp.max(logw[h], axis=1, keepdims=True)) for h in heads]
    w = [jnp.exp(logw[h] - m_t[h]) for h in heads]
    a = [jnp.exp(log_inter[h] - m_t[h]) for h in heads]
    q = [q_ref[:, sk[h]] for h in heads]
    k = [k_ref[:, sk[h]] * dk ** -0.5 for h in heads]
    qb = [x.astype(BF16) for x in q]
    vb = [v_ref[:, sv[h]].astype(BF16) for h in heads]
    s = [_dot_nt(qb[h], k[h].astype(BF16)) * w[h] for h in heads]
    cm = [c_ref[h] for h in heads]
    n = [n_ref[h:h + 1, :] for h in heads]
    num = [_dot(s[h].astype(BF16), vb[h]) + a[h] * _dot(qb[h], cm[h].astype(BF16)) for h in heads]
    den = [jnp.sum(s[h], axis=1, keepdims=True) + a[h] * jnp.sum(q[h] * n[h], axis=1, keepdims=True) for h in heads]
    hid = [num[h] / jnp.maximum(jnp.abs(den[h]), jnp.exp(-m_t[h])) for h in heads]
    m_new = [x[c - 1:c, :] for x in m_t]
    f_last = [x[c - 1:c, :] for x in fcum_col]
    a_end = [jnp.exp(f_last[h] + m_prev[h] - m_new[h]) for h in heads]
    kw = [jnp.exp(f_last[h] - fcum_col[h] + i_col[h] - m_new[h]) * k[h] for h in heads]
    c_new = [a_end[h] * cm[h] + _dot_tn(kw[h].astype(BF16), vb[h]) for h in heads]
    n_new = [a_end[h] * n[h] + jnp.sum(kw[h], axis=0, keepdims=True) for h in heads]
    for h in heads:
        c_ref[h] = c_new[h]
        n_ref[h:h + 1, :] = n_new[h]
        m_ref[h:h + 1, :] = m_new[h]
        out_ref[:, sv[h]] = (og_ref[:, sv[h]] * (_head_norm(hid[h]) * go_ref[...])).astype(out_ref.dtype)


def _ml_sample_body(q_ref, k_ref, v_ref, og_ref, i_ref, f_ref, go_ref, c0_ref, n0_ref, m0_ref,
                    out_ref, c_ref, n_ref, m_ref, *, n_heads, dk, dv, ls):
    rows = q_ref.shape[0]
    masks = _seq_masks(rows, ls)
    for hh in range(n_heads):
        sk = slice(hh * dk, (hh + 1) * dk)
        sv = slice(hh * dv, (hh + 1) * dv)
        out = jnp.zeros((rows, dv), F32)
        for bb, m in enumerate(masks):
            z = lambda t: jnp.where(m, t, 0.0)
            h, c_new, n_new, m_new = _ml_one(z(q_ref[:, sk]), z(k_ref[:, sk]) * dk ** -0.5, z(v_ref[:, sv]),
                                             i_ref[bb, hh:hh + 1, :], f_ref[bb, hh:hh + 1, :],
                                             c0_ref[bb, hh], n0_ref[bb, hh:hh + 1, :], m0_ref[bb, hh:hh + 1, :])
            c_ref[bb, hh] = c_new
            n_ref[bb, hh:hh + 1, :] = n_new
            m_ref[bb, hh:hh + 1, :] = m_new
            out = jnp.where(m, _head_norm(h) * go_ref[...], out)
        out_ref[:, sv] = og_ref[:, sv] * out


def _chunking(tok):
    c = CHUNK if tok.lp % CHUNK == 0 else tok.lp
    return c, tok.lp // c


def _pair_rows(tok):
    assert _SUBLANES % tok.ls == 0 and tok.mp % _SUBLANES == 0
    nb = _SUBLANES // tok.ls
    assert tok.bs % nb == 0
    return nb, tok.mp // _SUBLANES


def _sample_schedule(n_prompt_steps, pairs, nh, min_group):
    for gs in range(min_group, nh + 1):
        n_sample_steps = pairs * (nh // gs)
        if nh % gs == 0 and n_prompt_steps % n_sample_steps == 0:
            return gs, n_prompt_steps // n_sample_steps
    raise ValueError((n_prompt_steps, pairs, nh))


def _gla_fused_body(qp, kp, vp, lap, rp, go, qs, ks, vs, las, rs, s0, *rest, gp, gs, dk, dv, ls, ratio, nhg_p, nc):
    og_p, s_p, og_s, s_s = rest[-4:]
    _gla_prompt_body(qp, kp, vp, lap, rp, go, og_p, s_p, n_heads=gp, dk=dk, dv=dv)
    t = (pl.program_id(0) * nhg_p + pl.program_id(1)) * nc + pl.program_id(2)

    @pl.when(t % ratio == 0)
    def _():
        _gla_sample_body(qs, ks, vs, las, rs, go, s0, og_s, s_s, n_heads=gs, dk=dk, dv=dv, ls=ls)


def _gla_mix(q, k, v, la, r, g_o, state, layer, prev_states, tok, group_p):
    nl, bs, nh, dk, dv = state.shape
    c, nc = _chunking(tok)
    gp = group_p
    nhg_p = nh // gp
    nb, t0 = _pair_rows(tok)
    gs, ratio = _sample_schedule(tok.bp * nhg_p * nc, bs // nb, nh, 1)
    nhg_s = nh // gs

    def smp(b, hg, ci):
        u = ((b * nhg_p + hg) * nc + ci) // ratio
        return u // nhg_s, u % nhg_s

    def at_sample(f):
        return lambda b, hg, ci: f(*smp(b, hg, ci))

    tok_blk = lambda w: pl.BlockSpec((c, gp * w), lambda b, hg, ci: (b * nc + ci, hg))
    smp_blk = lambda w: pl.BlockSpec((_SUBLANES, gs * w), at_sample(lambda p, h: (t0 + p, h)))
    st_blk = pl.BlockSpec((None, nb, gs, dk, dv), at_sample(lambda p, h: (layer, p, h, 0, 0)))
    go = g_o.reshape(1, dv)
    args = [q, k, v, la, r, go, q, k, v, la, r, state]
    in_specs = [tok_blk(dk), tok_blk(dk), tok_blk(dv), tok_blk(dk), tok_blk(dv),
                pl.BlockSpec((1, dv), lambda b, hg, ci: (0, 0)),
                smp_blk(dk), smp_blk(dk), smp_blk(dv), smp_blk(dk), smp_blk(dv), st_blk]
    aliases = {}
    if prev_states is not None:
        args += list(prev_states)
        in_specs += [pl.BlockSpec(memory_space=pl.ANY)] * 2
        aliases = {len(args) - 2: 1, len(args) - 1: 3}
    og_p, s_p, og_s, s_s = pl.pallas_call(
        functools.partial(_gla_fused_body, gp=gp, gs=gs, dk=dk, dv=dv, ls=tok.ls, ratio=ratio, nhg_p=nhg_p, nc=nc),
        grid=(tok.bp, nhg_p, nc),
        in_specs=in_specs,
        out_specs=[tok_blk(dv), pl.BlockSpec((None, None, gp, dk, dv), lambda b, hg, ci: (layer, b, hg, 0, 0)),
                   pl.BlockSpec((_SUBLANES, gs * dv), at_sample(lambda p, h: (p, h))), st_blk],
        out_shape=(jax.ShapeDtypeStruct((tok.mp, nh * dv), BF16),
                   jax.ShapeDtypeStruct((nl, tok.bp, nh, dk, dv), F32),
                   jax.ShapeDtypeStruct((tok.ms, nh * dv), F32), jax.ShapeDtypeStruct(state.shape, F32)),
        input_output_aliases=aliases,
        compiler_params=_params(3),
        name="gla_mix",
    )(*args)
    return (og_p, og_s.astype(BF16)), (s_p, s_s)


def _rope_tables(pos, half):
    inv = ROPE_BASE ** (-jnp.arange(half, dtype=F32) / half)
    ang = pos[:, None] * inv[None, :]
    return jnp.cos(ang), jnp.sin(ang)


def _ret_fused_body(lg, qp, kp, vp, gp_ref, cosp, sinp, qs, ks, vs, gs_ref, coss, sins, s0, og_p, s_p, og_s, s_s, *,
                    gp, gs, dk, dv, ls, ratio, nhg_p, nhg_s, nc):
    _ret_prompt_body(lg, qp, kp, vp, gp_ref, cosp, sinp, og_p, s_p, n_heads=gp, dk=dk, dv=dv)
    t = (pl.program_id(0) * nhg_p + pl.program_id(1)) * nc + pl.program_id(2)

    @pl.when(t % ratio == 0)
    def _():
        _ret_sample_body(lg, qs, ks, vs, gs_ref, coss, sins, s0, og_s, s_s,
                         head0=((t // ratio) % nhg_s) * gs, n_heads=gs, dk=dk, dv=dv, ls=ls)


def _ret_mix(q, k, v, g, state, tok, group_p):
    _, bs, nh, dk, dv = state.shape
    c, nc = _chunking(tok)
    half = dk // 2
    log_gamma = jnp.log1p(-jnp.exp2(-5.0 - jnp.arange(nh, dtype=F32)))
    cos_p, sin_p = _rope_tables(0 + jnp.arange(tok.lp, dtype=F32), half)
    cos_s, sin_s = _rope_tables(PAST_LEN + jnp.arange(tok.ls, dtype=F32), half)
    gp = group_p
    nhg_p = nh // gp
    nb, t0 = _pair_rows(tok)
    gs, ratio = _sample_schedule(tok.bp * nhg_p * nc, bs // nb, nh, 1)
    nhg_s = nh // gs

    def at_sample(f):
        def index(b, hg, ci, lg):
            u = ((b * nhg_p + hg) * nc + ci) // ratio
            return f(u // nhg_s, u % nhg_s)
        return index

    tok_blk = lambda w: pl.BlockSpec((c, gp * w), lambda b, hg, ci, lg: (b * nc + ci, hg))
    rope_blk = pl.BlockSpec((c, half), lambda b, hg, ci, lg: (ci, 0))
    smp_blk = lambda w: pl.BlockSpec((_SUBLANES, gs * w), at_sample(lambda p, h: (t0 + p, h)))
    st_blk = pl.BlockSpec((None, nb, gs, dk, dv), at_sample(lambda p, h: (0, p, h, 0, 0)))
    rope_s = pl.BlockSpec((_SUBLANES, half), lambda b, hg, ci, lg: (0, 0))
    og_p, s_p, og_s, s_s = pl.pallas_call(
        functools.partial(_ret_fused_body, gp=gp, gs=gs, dk=dk, dv=dv, ls=tok.ls, ratio=ratio, nhg_p=nhg_p,
                          nhg_s=nhg_s, nc=nc),
        grid_spec=pltpu.PrefetchScalarGridSpec(
            num_scalar_prefetch=1,
            grid=(tok.bp, nhg_p, nc),
            in_specs=[tok_blk(dk), tok_blk(dk), tok_blk(dv), tok_blk(dv), rope_blk, rope_blk,
                      smp_blk(dk), smp_blk(dk), smp_blk(dv), smp_blk(dv), rope_s, rope_s, st_blk],
            out_specs=[tok_blk(dv), pl.BlockSpec((None, None, gp, dk, dv), lambda b, hg, ci, lg: (0, b, hg, 0, 0)),
                       pl.BlockSpec((_SUBLANES, gs * dv), at_sample(lambda p, h: (p, h))), st_blk],
        ),
        out_shape=(jax.ShapeDtypeStruct((tok.mp, nh * dv), BF16), jax.ShapeDtypeStruct((1, tok.bp, nh, dk, dv), F32),
                   jax.ShapeDtypeStruct((tok.ms, nh * dv), F32), jax.ShapeDtypeStruct(state.shape, F32)),
        compiler_params=_params(3),
        name="ret_mix",
    )(log_gamma, q, k, v, g, cos_p, sin_p, q, k, v, g, jnp.tile(cos_s, (nb, 1)), jnp.tile(sin_s, (nb, 1)), state)
    return (og_p, og_s.astype(BF16)), s_p, s_s


def _ml_mix(q, k, v, og, gates, g_o, c0, n0, m0, tok):
    _, bs, nh, dk, dv = c0.shape
    c, nc = _chunking(tok)
    nb, t0 = _pair_rows(tok)
    g_p = gates[:tok.mp, :2 * nh].reshape(tok.bp, nc, c, 2 * nh).transpose(0, 1, 3, 2)
    g_s = gates[tok.mp:, :2 * nh].reshape(bs, tok.ls, 2 * nh).transpose(0, 2, 1)
    own = (jnp.arange(nb)[None, :] == (jnp.arange(bs) % nb)[:, None])[:, None, :, None]
    i_s = jnp.where(own, g_s[:, :nh, None, :], -jnp.inf).reshape(bs, nh, _SUBLANES)
    f_s = jnp.where(own, g_s[:, nh:, None, :], 0.0).reshape(bs, nh, _SUBLANES)
    go = g_o.reshape(1, dv)

    tok_blk = lambda w: pl.BlockSpec((c, nh * w), lambda b, ci: (b * nc + ci, 0))
    gate_blk = lambda first: pl.BlockSpec((None, None, nh, c), lambda b, ci: (b, ci, first, 0))
    out_p, c_p, n_p, m_p = pl.pallas_call(
        functools.partial(_ml_prompt_body, n_heads=nh, dk=dk, dv=dv),
        grid=(tok.bp, nc),
        in_specs=[tok_blk(dk), tok_blk(dk), tok_blk(dv), tok_blk(dv), gate_blk(0), gate_blk(1),
                  pl.BlockSpec((1, dv), lambda b, ci: (0, 0))],
        out_specs=[tok_blk(dv),
                   pl.BlockSpec((None, None, nh, dk, dv), lambda b, ci: (0, b, 0, 0, 0)),
                   pl.BlockSpec((None, None, nh, dk), lambda b, ci: (0, b, 0, 0)),
                   pl.BlockSpec((None, None, nh, 1), lambda b, ci: (0, b, 0, 0))],
        out_shape=(jax.ShapeDtypeStruct((tok.mp, nh * dv), BF16),
                   jax.ShapeDtypeStruct((1, tok.bp, nh, dk, dv), F32),
                   jax.ShapeDtypeStruct((1, tok.bp, nh, dk), F32),
                   jax.ShapeDtypeStruct((1, tok.bp, nh, 1), F32)),
        compiler_params=_params(2),
        name="ml_prompt",
    )(q, k, v, og, g_p, g_p, go)

    smp_blk = lambda w: pl.BlockSpec((_SUBLANES, nh * w), lambda b: (t0 + b, 0))
    sgate_blk = pl.BlockSpec((nb, nh, _SUBLANES), lambda b: (b, 0, 0))
    c_blk = pl.BlockSpec((None, nb, nh, dk, dv), lambda b: (0, b, 0, 0, 0))
    n_blk = pl.BlockSpec((None, nb, nh, dk), lambda b: (0, b, 0, 0))
    m_blk = pl.BlockSpec((None, nb, nh, 1), lambda b: (0, b, 0, 0))
    out_s, c_s, n_s, m_s = pl.pallas_call(
        functools.partial(_ml_sample_body, n_heads=nh, dk=dk, dv=dv, ls=tok.ls),
        grid=(bs // nb,),
        in_specs=[smp_blk(dk), smp_blk(dk), smp_blk(dv), smp_blk(dv), sgate_blk, sgate_blk,
                  pl.BlockSpec((1, dv), lambda b: (0, 0)), c_blk, n_blk, m_blk],
        out_specs=[pl.BlockSpec((_SUBLANES, nh * dv), lambda b: (b, 0)), c_blk, n_blk, m_blk],
        out_shape=(jax.ShapeDtypeStruct((tok.ms, nh * dv), F32),
                   jax.ShapeDtypeStruct(c0.shape, F32), jax.ShapeDtypeStruct(n0.shape, F32),
                   jax.ShapeDtypeStruct(m0.shape + (1,), F32)),
        compiler_params=_params(1),
        name="ml_sample",
    )(q, k, v, og, i_s, f_s, go, c0, n0, m0[..., None])
    return (out_p, out_s.astype(BF16)), (c_p, n_p, m_p[..., 0]), (c_s, n_s, m_s[..., 0])


def kernel(x_prompt, x_sample, state_gla, state_ret, state_mlstm_C, state_mlstm_n, state_mlstm_m, c_prompt, c_sample, w_ada, b_ada, g_mix, g_mlp, w_up, w_down, g_final, w_gla_q, w_gla_k, w_gla_v, w_gla_a1, w_gla_a2, b_gla_a, w_gla_r, b_gla_r, g_gla_o, w_gla_o, w_ret_q, w_ret_k, w_ret_v, w_ret_g, w_ret_o, w_ml_q, w_ml_k, w_ml_v, w_ml_i, b_ml_i, w_ml_f, b_ml_f, w_ml_og, g_ml_o, w_ml_o):
    bp, lp, d = x_prompt.shape
    bs, ls, _ = x_sample.shape
    depth = w_ada.shape[0]
    ms = bs * ls
    tok = _Tok(bp, lp, bs, ls, min(_MM_TILE, ms))
    ntok = _Tok(bp, lp, bs, ls, min(_NORM_TILE, ms))
    tn = min(_MM_TN, d)

    x = (x_prompt.reshape(bp * lp, d), x_sample.reshape(ms, d))

    n_c = bs + bp
    rows = -(-n_c // 16) * 16
    c_all = jnp.pad(jnp.concatenate([c_sample, c_prompt], axis=0), ((0, rows - n_c), (0, 0)))
    mod = _Mod(_ada_mod(c_all, w_ada, b_ada, bs, ls, min(_ADA_TN, d)), bp, ms, d)

    w_down_last = _last_chunk_bf16(w_down, d)
    w_o_last = [_last_chunk_bf16(w, d) for w in (w_gla_o, w_ret_o, w_ml_o)]

    gla_states = None
    ret_p = ret_s = ml_p = ml_s = None
    for i in range(depth):
        kind, j = i % 3, i // 3
        h = _norm_mod(x, g_mix[i], mod, i, 1, 0, ntok)
        if kind == 0:
            q = _mm(h, w_gla_q, j, tok, tn=tn, name="gla_q")
            k = _mm(h, w_gla_k, j, tok, tn=tn, name="gla_k")
            v = _mm(h, w_gla_v, j, tok, tn=tn, name="gla_v")
            r = _mm(h, w_gla_r, j, tok, tn=tn, epi=_epi_bias_silu, extras=_bias_extras(b_gla_r[j], tn), name="gla_r")
            la = _gla_decay(h, w_gla_a1[j], w_gla_a2[j], b_gla_a[j], tok)
            nh = state_gla.shape[2]
            og, gla_states = _gla_mix(q, k, v, la, r, g_gla_o[j], state_gla, j, gla_states, tok, min(2, nh))
            w_o = w_gla_o
        elif kind == 1:
            q = _mm(h, w_ret_q, j, tok, tn=tn, name="ret_q")
            k = _mm(h, w_ret_k, j, tok, tn=tn, name="ret_k")
            v = _mm(h, w_ret_v, j, tok, tn=tn, name="ret_v")
            g = _mm(h, w_ret_g, j, tok, tn=tn, epi=_epi_silu, name="ret_g")
            nh = state_ret.shape[2]
            og, ret_p, ret_s = _ret_mix(q, k, v, g, state_ret, tok, min(8, nh))
            w_o = w_ret_o
        else:
            q = _mm(h, w_ml_q, j, tok, tn=tn, name="ml_q")
            k = _mm(h, w_ml_k, j, tok, tn=tn, name="ml_k")
            v = _mm(h, w_ml_v, j, tok, tn=tn, name="ml_v")
            ogate = _mm(h, w_ml_og, j, tok, tn=tn, epi=_epi_sigmoid, name="ml_og")
            gates = _ml_gates(h, w_ml_i[j], b_ml_i[j], w_ml_f[j], b_ml_f[j], tok)
            og, ml_p, ml_s = _ml_mix(q, k, v, ogate, gates, g_ml_o[j], state_mlstm_C, state_mlstm_n,
                                     state_mlstm_m, tok)
            w_o = w_ml_o
        x = _mm_resid(og, w_o, w_o_last[kind], j, x, mod, i, 2, tok, d, name="mix_o")
        hf = _norm_mod(x, g_mlp[i], mod, i, 4, 3, ntok)
        hid = _mm(hf, w_up, i, tok, tn=tn, epi=_epi_relu2, out_dtype=BF16, name="mlp_up")
        x = _mm_resid(hid, w_down, w_down_last, i, x, mod, i, 5, tok, d, name="mlp_down")

    y_p = _final_norm(x, g_final, ntok.tile, 0, ntok.npt).reshape(bp, lp, d)
    y_s = _final_norm(x, g_final, ntok.tile, ntok.npt, ntok.nst).reshape(bs, ls, d)
    return (y_p, y_s, gla_states[0], gla_states[1], ret_p, ret_s,
            ml_p[0], ml_s[0], ml_p[1], ml_s[1], ml_p[2], ml_s[2])
```
